```python
import math
import jax, jax.numpy as jnp
from jax import lax
import numpy as np

D_MODEL = 2048
BATCH = 4
SEQ = 8192
DEPTH = 1
DEC_BATCH = 32
DEC_SEQ = 64
PAST_LEN = 4096

CHUNK = 64
BLOCK_Q = 128
D_MIX = D_MODEL
SB_HEADS = 8
SB_HD = 128
SB_WIDTH = SB_HEADS * SB_HD
DA_HEADS = 4
DA_HD = 128
DA_VD = 2 * DA_HD
DA_QK_WIDTH = DA_HEADS * 2 * DA_HD
DA_WIDTH = DA_HEADS * DA_VD
D_IN_PROJ = 3 * SB_WIDTH + 2 * DA_QK_WIDTH + DA_WIDTH
SPLIT_POINTS = [SB_WIDTH, 2 * SB_WIDTH, 3 * SB_WIDTH,
                3 * SB_WIDTH + DA_QK_WIDTH, 3 * SB_WIDTH + 2 * DA_QK_WIDTH]
N_GROUPS = 4
EXPERTS_PER_GROUP = 4
N_EXPERTS = N_GROUPS * EXPERTS_PER_GROUP
TOP_K = 2
D_EXPERT = D_MODEL // 4
LN_EPS = 1e-5
RMS_EPS = 1e-5
NEG_INF = -1e30
ALPHA = (2 * DEPTH) ** 0.25
BETA = (8 * DEPTH) ** -0.25

kernel_name = "hybrid_stickbreak_diffattn_hmoe_stream_step"


def _layer_norm(x, g, b):
    xf = x.astype(jnp.float32)
    mu = jnp.mean(xf, axis=-1, keepdims=True)
    var = jnp.mean(jnp.square(xf - mu), axis=-1, keepdims=True)
    y = (xf - mu) * lax.rsqrt(var + LN_EPS) * g.astype(jnp.float32) + b.astype(jnp.float32)
    return y.astype(x.dtype)


def _rms_norm(x, g):
    xf = x.astype(jnp.float32)
    y = xf * lax.rsqrt(jnp.mean(jnp.square(xf), axis=-1, keepdims=True) + RMS_EPS)
    return (y * g.astype(jnp.float32)).astype(x.dtype)


def _alibi_slopes(n):
    return jnp.asarray([2.0 ** (-8.0 * (h + 1) / n) for h in range(n)], dtype=jnp.float32)


def _stick_breaking_block(q, k, v, q_pos, k_pos):
    z = jnp.einsum("bqhd,bshd->bhqs", q, k).astype(jnp.float32) * (SB_HD ** -0.5)
    mask = k_pos[None, :] < q_pos[:, None]
    log_not = jnp.where(mask, jax.nn.log_sigmoid(-z), 0.0)
    between = lax.cumsum(log_not, axis=3, reverse=True) - log_not
    w = jnp.where(mask, jnp.exp(jax.nn.log_sigmoid(z) + between), 0.0)
    return jnp.einsum("bhqs,bshd->bqhd", w.astype(v.dtype), v)


def _diff_attn_block(q, k, v, q_pos, k_pos, lam, slopes):
    s = jnp.einsum("bqhmd,bshmd->bhmqs", q, k).astype(jnp.float32) * (DA_HD ** -0.5)
    dist = jnp.abs(q_pos[:, None] - k_pos[None, :]).astype(jnp.float32)
    visible = (k_pos[None, :] // CHUNK) <= (q_pos[:, None] // CHUNK)
    s = jnp.where(visible, s - slopes[None, :, None, None, None] * dist, NEG_INF)
    p = jax.nn.softmax(s, axis=-1)
    w = p[:, :, 0] - lam * p[:, :, 1]
    return jnp.einsum("bhqs,bshv->bqhv", w.astype(v.dtype), v)


def _query_blocks(fn, q, q_pos):
    b, s = q.shape[0], q.shape[1]
    if s <= BLOCK_Q:
        return fn(q, q_pos)
    nb = s // BLOCK_Q
    qb = jnp.moveaxis(q.reshape((b, nb, BLOCK_Q) + q.shape[2:]), 1, 0)
    pb = q_pos.reshape(nb, BLOCK_Q)
    out = lax.map(lambda a: fn(a[0], a[1]), (qb, pb))
    return jnp.moveaxis(out, 0, 1).reshape((b, s) + out.shape[3:])


def _mixer(x, q_pos, k_pos, past, w_in, w_out, lq1, lk1, lq2, lk2, sub_g, lam_init, slopes):
    b, s, _ = x.shape
    proj = jnp.einsum("bsd,de->bse", x, w_in)
    sq, sk, sv, dq, dk, dv = jnp.split(proj, SPLIT_POINTS, axis=-1)
    sq = sq.reshape(b, s, SB_HEADS, SB_HD)
    sk = sk.reshape(b, s, SB_HEADS, SB_HD)
    sv = sv.reshape(b, s, SB_HEADS, SB_HD)
    dq = dq.reshape(b, s, DA_HEADS, 2, DA_HD)
    dk = dk.reshape(b, s, DA_HEADS, 2, DA_HD)
    dv = dv.reshape(b, s, DA_HEADS, DA_VD)
    new_rows = (sk, sv, dk, dv)
    if past is not None:
        sk = jnp.concatenate([past[0].astype(sk.dtype), sk], axis=1)
        sv = jnp.concatenate([past[1].astype(sv.dtype), sv], axis=1)
        dk = jnp.concatenate([past[2].astype(dk.dtype), dk], axis=1)
        dv = jnp.concatenate([past[3].astype(dv.dtype), dv], axis=1)
    lam = (jnp.exp(jnp.sum(lq1.astype(jnp.float32) * lk1.astype(jnp.float32)))
           - jnp.exp(jnp.sum(lq2.astype(jnp.float32) * lk2.astype(jnp.float32))) + lam_init)
    sb_o = _query_blocks(lambda qb, pb: _stick_breaking_block(qb, sk, sv, pb, k_pos), sq, q_pos)
    da_o = _query_blocks(lambda qb, pb: _diff_attn_block(qb, dk, dv, pb, k_pos, lam, slopes), dq, q_pos)
    da_o = _rms_norm(da_o, sub_g) * (1.0 - lam_init)
    merged = jnp.concatenate([sb_o.reshape(b, s, SB_WIDTH), da_o.reshape(b, s, DA_WIDTH)], axis=-1)
    return jnp.einsum("bse,ed->bsd", merged, w_out), new_rows


def _hmoe(x, w_coarse, b_coarse, w_fine, b_fine, w_gate, w_up, w_down):
    b, s, d = x.shape
    xt = x.reshape(b * s, d)
    coarse = (xt @ w_coarse + b_coarse).astype(jnp.float32)
    p_group = jax.nn.softmax(coarse, axis=-1)
    g_sel = jnp.argmax(coarse, axis=-1)
    g_gate = jnp.max(p_group, axis=-1)
    g_onehot = jax.nn.one_hot(g_sel, N_GROUPS, dtype=jnp.float32)
    fine = (jnp.einsum("td,dge->tge", xt, w_fine) + b_fine).astype(jnp.float32)
    fine_sel = jnp.einsum("tge,tg->te", fine, g_onehot)
    top_v, top_i = lax.top_k(fine_sel, TOP_K)
    top_w = jax.nn.softmax(top_v, axis=-1) * g_gate[:, None]
    within = jnp.sum(jax.nn.one_hot(top_i, EXPERTS_PER_GROUP, dtype=jnp.float32) * top_w[..., None], axis=1)
    comb = (g_onehot[:, :, None] * within[:, None, :]).reshape(b * s, N_EXPERTS).astype(x.dtype)
    y = jnp.zeros_like(xt)
    for e in range(N_EXPERTS):
        h = jax.nn.silu(xt @ w_gate[e]) * (xt @ w_up[e])
        y = y + comb[:, e:e + 1] * (h @ w_down[e])
    return y.reshape(b, s, d)


def setup_inputs(seed: int = 0) -> dict:
    key = jax.random.key(seed)
    ks = jax.random.split(key, 24)
    f32 = jnp.float32
    col_scale = jnp.concatenate([
        jnp.ones((2 * SB_WIDTH,), f32), BETA * jnp.ones((SB_WIDTH,), f32),
        jnp.ones((2 * DA_QK_WIDTH,), f32), BETA * jnp.ones((DA_WIDTH,), f32)])
    return {
        "x_prompt": jax.random.normal(ks[0], (BATCH, SEQ, D_MODEL), f32),
        "x_sample": jax.random.normal(ks[1], (DEC_BATCH, DEC_SEQ, D_MODEL), f32),
        "cache_sb_k": jax.random.normal(ks[2], (DEPTH, DEC_BATCH, PAST_LEN, SB_HEADS, SB_HD), f32),
        "cache_sb_v": BETA * jax.random.normal(ks[3], (DEPTH, DEC_BATCH, PAST_LEN, SB_HEADS, SB_HD), f32),
        "cache_da_k": jax.random.normal(ks[4], (DEPTH, DEC_BATCH, PAST_LEN, DA_HEADS, 2, DA_HD), f32),
        "cache_da_v": BETA * jax.random.normal(ks[5], (DEPTH, DEC_BATCH, PAST_LEN, DA_HEADS, DA_VD), f32),
        "w_in": jax.random.normal(ks[6], (DEPTH, D_MODEL, D_IN_PROJ), f32) * (D_MODEL ** -0.5) * col_scale,
        "w_out": jax.random.normal(ks[7], (DEPTH, D_MIX, D_MODEL), f32) * (D_MIX ** -0.5) * BETA,
        "lambda_q1": 0.1 * jax.random.normal(ks[8], (DEPTH, DA_HD), f32),
        "lambda_k1": 0.1 * jax.random.normal(ks[9], (DEPTH, DA_HD), f32),
        "lambda_q2": 0.1 * jax.random.normal(ks[10], (DEPTH, DA_HD), f32),
        "lambda_k2": 0.1 * jax.random.normal(ks[11], (DEPTH, DA_HD), f32),
        "subln_g": 1.0 + 0.02 * jax.random.normal(ks[12], (DEPTH, DA_VD), f32),
        "ln1_g": 1.0 + 0.02 * jax.random.normal(ks[13], (DEPTH, D_MODEL), f32),
        "ln1_b": 0.02 * jax.random.normal(ks[14], (DEPTH, D_MODEL), f32),
        "w_coarse": jax.random.normal(ks[15], (DEPTH, D_MODEL, N_GROUPS), f32) * (D_MODEL ** -0.5),
        "b_coarse": 0.01 * jax.random.normal(ks[16], (DEPTH, N_GROUPS), f32),
        "w_fine": jax.random.normal(ks[17], (DEPTH, D_MODEL, N_GROUPS, EXPERTS_PER_GROUP), f32) * (D_MODEL ** -0.5),
        "b_fine": 0.01 * jax.random.normal(ks[18], (DEPTH, N_GROUPS, EXPERTS_PER_GROUP), f32),
        "w_gate": jax.random.normal(ks[19], (DEPTH, N_EXPERTS, D_MODEL, D_EXPERT), f32) * (D_MODEL ** -0.5),
        "w_up": jax.random.normal(ks[20], (DEPTH, N_EXPERTS, D_MODEL, D_EXPERT), f32) * (D_MODEL ** -0.5) * BETA,
        "w_down": jax.random.normal(ks[21], (DEPTH, N_EXPERTS, D_EXPERT, D_MODEL), f32) * (D_EXPERT ** -0.5) * BETA,
        "ln2_g": 1.0 + 0.02 * jax.random.normal(ks[22], (DEPTH, D_MODEL), f32),
        "ln2_b": 0.02 * jax.random.normal(ks[23], (DEPTH, D_MODEL), f32),
    }


def reference(x_prompt, x_sample, cache_sb_k, cache_sb_v, cache_da_k, cache_da_v,
              w_in, w_out, lambda_q1, lambda_k1, lambda_q2, lambda_k2, subln_g,
              ln1_g, ln1_b, w_coarse, b_coarse, w_fine, b_fine, w_gate, w_up, w_down,
              ln2_g, ln2_b):
    slopes = _alibi_slopes(DA_HEADS)
    s_p = x_prompt.shape[1]
    s_s = x_sample.shape[1]
    past_len = cache_sb_k.shape[2]
    pos_p = jnp.arange(s_p, dtype=jnp.int32)
    pos_s = past_len + jnp.arange(s_s, dtype=jnp.int32)
    kpos_s = jnp.arange(past_len + s_s, dtype=jnp.int32)
    hp, hs = x_prompt, x_sample
    rows_p, rows_s = [], []
    for l in range(DEPTH):
        lam_init = 0.8 - 0.6 * math.exp(-0.3 * l)
        mix_w = (w_in[l], w_out[l], lambda_q1[l], lambda_k1[l], lambda_q2[l], lambda_k2[l],
                 subln_g[l], lam_init, slopes)
        a_p, r_p = _mixer(hp, pos_p, pos_p, None, *mix_w)
        past = (cache_sb_k[l], cache_sb_v[l], cache_da_k[l], cache_da_v[l])
        a_s, r_s = _mixer(hs, pos_s, kpos_s, past, *mix_w)
        hp = _layer_norm(ALPHA * hp + a_p, ln1_g[l], ln1_b[l])
        hs = _layer_norm(ALPHA * hs + a_s, ln1_g[l], ln1_b[l])
        moe_w = (w_coarse[l], b_coarse[l], w_fine[l], b_fine[l], w_gate[l], w_up[l], w_down[l])
        hp = _layer_norm(ALPHA * hp + _hmoe(hp, *moe_w), ln2_g[l], ln2_b[l])
        hs = _layer_norm(ALPHA * hs + _hmoe(hs, *moe_w), ln2_g[l], ln2_b[l])
        rows_p.append(r_p)
        rows_s.append(r_s)
    new_sb_k_prompt = jnp.stack([r[0] for r in rows_p], axis=0)
    new_sb_v_prompt = jnp.stack([r[1] for r in rows_p], axis=0)
    new_da_k_prompt = jnp.stack([r[2] for r in rows_p], axis=0)
    new_da_v_prompt = jnp.stack([r[3] for r in rows_p], axis=0)
    new_sb_k_sample = jnp.stack([r[0] for r in rows_s], axis=0)
    new_sb_v_sample = jnp.stack([r[1] for r in rows_s], axis=0)
    new_da_k_sample = jnp.stack([r[2] for r in rows_s], axis=0)
    new_da_v_sample = jnp.stack([r[3] for r in rows_s], axis=0)
    return (hp, hs, new_sb_k_prompt, new_sb_v_prompt, new_da_k_prompt, new_da_v_prompt,
            new_sb_k_sample, new_sb_v_sample, new_da_k_sample, new_da_v_sample)
```

```python
import functools
import math

import jax
import jax.numpy as jnp
from jax import lax
from jax.experimental import pallas as pl
from jax.experimental.pallas import tpu as pltpu

F32 = jnp.float32
BF16 = jnp.bfloat16

CHUNK = 64
LN_EPS = 1e-5
RMS_EPS = 1e-5
NEG_INF = -1e30
TOP_K = 2
LANES = 128
SUBLANES = 8
VMEM_CAP = 64 * 2**20


def _vmem_limit(block_bytes, scratch_bytes=0):
    est = 2 * block_bytes + scratch_bytes + 16 * 2**20
    return int(min(est, VMEM_CAP - 6 * 2**20))


def _dot(a, b):
    return jnp.dot(a, b, preferred_element_type=F32)


def _dot_nt(a, b):
    return lax.dot_general(a, b, (((1,), (1,)), ((), ())), preferred_element_type=F32)


def _layer_norm(y, g, b):
    mu = jnp.mean(y, axis=1, keepdims=True)
    yc = y - mu
    var = jnp.mean(yc * yc, axis=1, keepdims=True)
    return yc * lax.rsqrt(var + LN_EPS) * g + b


def _proj_kernel(x_ref, w_ref, o16_ref, *o32_refs, slots):
    acc = _dot(x_ref[...].astype(BF16), w_ref[...])
    o16_ref[...] = acc.astype(BF16)
    for o32_ref in o32_refs:
        for j, cb in enumerate(slots):
            o32_ref[pl.ds(j, acc.shape[0], stride=SUBLANES), :] = acc[:, cb * LANES:(cb + 1) * LANES]


def _proj(x, w, col_block, slots, name):
    t, d = x.shape
    width = SUBLANES * LANES
    tm = min(512, t)
    assert t % tm == 0
    blk = tm * d * 4 + d * width * 2 + tm * width * (2 + (4 if slots else 0))
    out_specs = [pl.BlockSpec((tm, width), lambda i: (i, 0))]
    out_shape = [jax.ShapeDtypeStruct((t, width), BF16)]
    if slots:
        out_specs.append(pl.BlockSpec((tm * SUBLANES, LANES), lambda i: (i, 0)))
        out_shape.append(jax.ShapeDtypeStruct((t * SUBLANES, LANES), F32))
    return pl.pallas_call(
        functools.partial(_proj_kernel, slots=slots),
        grid=(t // tm,),
        in_specs=[pl.BlockSpec((tm, d), lambda i: (i, 0)),
                  pl.BlockSpec((d, width), lambda i: (0, col_block))],
        out_specs=out_specs,
        out_shape=out_shape,
        compiler_params=pltpu.CompilerParams(
            dimension_semantics=("arbitrary",), vmem_limit_bytes=_vmem_limit(blk)),
        name=name,
    )(x, w)


def _tile_rows(ref, slot, n):
    return ref[pl.ds(slot, n, stride=SUBLANES), :]


def _suffix_matrix(n):
    r = lax.broadcasted_iota(jnp.int32, (n, n), 0)
    c = lax.broadcasted_iota(jnp.int32, (n, n), 1)
    return (r > c).astype(BF16)


def _strict_causal_mask(nq, nk):
    r = lax.broadcasted_iota(jnp.int32, (nq, nk), 0)
    c = lax.broadcasted_iota(jnp.int32, (nq, nk), 1)
    return c < r


def _sb_block(q, k, v, carry, acc, umat, mask, scale):
    z = _dot_nt(q, k) * scale
    log_not = jnp.minimum(-z, 0.0) - jnp.log(1.0 + jnp.exp(-jnp.abs(z)))
    log_beta = z + log_not
    if mask is not None:
        log_not = jnp.where(mask, log_not, 0.0)
    hi = log_not.astype(BF16)
    lo = (log_not - hi.astype(F32)).astype(BF16)
    suffix = _dot(hi, umat) + _dot(lo, umat)
    w = jnp.exp(log_beta + suffix + carry)
    if mask is not None:
        w = jnp.where(mask, w, 0.0)
    acc = acc + _dot(w.astype(BF16), v)
    carry = carry + suffix[:, :1] + log_not[:, :1]
    return carry, acc


def _sb_prompt_kernel(q_ref, k_ref, v_ref, o_ref, carry_ref, acc_ref, *, blk, scale):
    qi = pl.program_id(2)
    q = q_ref[...]
    umat = _suffix_matrix(blk)
    start = pl.multiple_of(qi * blk, blk)
    carry, acc = _sb_block(q, k_ref[pl.ds(start, blk), :], v_ref[pl.ds(start, blk), :],
                           jnp.zeros((blk, 1), F32), jnp.zeros(acc_ref.shape, F32),
                           umat, _strict_causal_mask(blk, blk), scale)
    carry_ref[...] = carry
    acc_ref[...] = acc

    def body(i, c):
        st = pl.multiple_of((qi - 1 - i) * blk, blk)
        carry, acc = _sb_block(q, k_ref[pl.ds(st, blk), :], v_ref[pl.ds(st, blk), :],
                               carry_ref[...], acc_ref[...], umat, None, scale)
        carry_ref[...] = carry
        acc_ref[...] = acc
        return c

    lax.fori_loop(0, qi, body, 0)
    o_ref[...] = acc_ref[...].astype(o_ref.dtype)


def _sb_prompt(q, k, v, batch, seq, heads, hd):
    blk = min(256, seq)
    assert seq % blk == 0
    nq = seq // blk
    blocks = blk * hd * 2 * 2 + 2 * seq * hd * 2
    return pl.pallas_call(
        functools.partial(_sb_prompt_kernel, blk=blk, scale=hd ** -0.5),
        grid=(batch, heads, nq),
        in_specs=[pl.BlockSpec((blk, hd), lambda b, h, i: (b * nq + i, h)),
                  pl.BlockSpec((seq, hd), lambda b, h, i: (b, h)),
                  pl.BlockSpec((seq, hd), lambda b, h, i: (b, h))],
        out_specs=pl.BlockSpec((blk, hd), lambda b, h, i: (b * nq + i, h)),
        out_shape=jax.ShapeDtypeStruct(q.shape, BF16),
        scratch_shapes=[pltpu.VMEM((blk, 1), F32), pltpu.VMEM((blk, hd), F32)],
        compiler_params=pltpu.CompilerParams(
            dimension_semantics=("arbitrary", "arbitrary", "arbitrary"),
            vmem_limit_bytes=_vmem_limit(blocks)),
        name="sb_prompt",
    )(q, k, v)


def _sb_sample_kernel(q_ref, kn_ref, vn_ref, kc_ref, vc_ref, o_ref, carry_ref, acc_ref, u_ref,
                      *, heads, hd, scale):
    kb = pl.program_id(1)
    nq = q_ref.shape[0]

    @pl.when(kb == 0)
    def _():
        u_ref[...] = _suffix_matrix(u_ref.shape[0])
        umat = _suffix_matrix(nq)
        mask = _strict_causal_mask(nq, nq)
        for h in range(heads):
            sl = slice(h * hd, (h + 1) * hd)
            carry, acc = _sb_block(q_ref[:, sl], kn_ref[:, sl], vn_ref[:, sl],
                                   jnp.zeros((nq, 1), F32), jnp.zeros((nq, hd), F32),
                                   umat, mask, scale)
            carry_ref[h] = carry
            acc_ref[:, sl] = acc

    @pl.when(kb > 0)
    def _():
        umat = u_ref[...]
        bk = u_ref.shape[0]
        for h in range(heads):
            sl = slice(h * hd, (h + 1) * hd)
            carry, acc = _sb_block(q_ref[:, sl], _tile_rows(kc_ref, h, bk).astype(BF16),
                                   _tile_rows(vc_ref, h, bk).astype(BF16),
                                   carry_ref[h], acc_ref[:, sl], umat, None, scale)
            carry_ref[h] = carry
            acc_ref[:, sl] = acc

    @pl.when(kb == pl.num_programs(1) - 1)
    def _():
        o_ref[...] = acc_ref[...].astype(o_ref.dtype)


def _sb_sample(q, kn, vn, kc, vc, batch, nq, past, heads, hd):
    assert heads == SUBLANES and hd == LANES
    width = heads * hd
    bk = min(512, past)
    assert past % bk == 0
    nkb = past // bk
    new_spec = pl.BlockSpec((nq, width), lambda b, j: (b, 0))
    cache_spec = pl.BlockSpec((bk * SUBLANES, LANES), lambda b, j: (b * nkb + nkb - jnp.maximum(j, 1), 0))
    blocks = 4 * nq * width * 2 + 2 * bk * width * 4
    scratch = heads * nq * LANES * 4 + nq * width * 4 + bk * bk * 2
    return pl.pallas_call(
        functools.partial(_sb_sample_kernel, heads=heads, hd=hd, scale=hd ** -0.5),
        grid=(batch, nkb + 1),
        in_specs=[new_spec, new_spec, new_spec, cache_spec, cache_spec],
        out_specs=new_spec,
        out_shape=jax.ShapeDtypeStruct(q.shape, BF16),
        scratch_shapes=[pltpu.VMEM((heads, nq, 1), F32), pltpu.VMEM((nq, width), F32),
                        pltpu.VMEM((bk, bk), BF16)],
        compiler_params=pltpu.CompilerParams(
            dimension_semantics=("arbitrary", "arbitrary"),
            vmem_limit_bytes=_vmem_limit(blocks, scratch)),
        name="sb_sample",
    )(q, kn, vn, kc, vc)


def _da_update(s, v, m_old, l_old, acc_old):
    m_new = jnp.maximum(m_old, jnp.max(s, axis=1, keepdims=True))
    alpha = jnp.exp(m_old - m_new)
    p = jnp.exp(s - m_new)
    l_new = alpha * l_old + jnp.sum(p, axis=1, keepdims=True)
    acc_new = alpha * acc_old + _dot(p.astype(BF16), v)
    return m_new, l_new, acc_new


def _da_same_block_bias(nq, nk, slope):
    r = lax.broadcasted_iota(jnp.int32, (nq, nk), 0)
    c = lax.broadcasted_iota(jnp.int32, (nq, nk), 1)
    return slope * (r - jnp.abs(r - c)).astype(F32)


def _da_chunk_mask(nq, nk):
    r = lax.broadcasted_iota(jnp.int32, (nq, nk), 0)
    c = lax.broadcasted_iota(jnp.int32, (nq, nk), 1)
    return (c // CHUNK) <= (r // CHUNK)


def _da_lambda(lam_ref, lam_init):
    lv = lam_ref[...]
    a = jnp.sum(lv[0:1] * lv[1:2], axis=1, keepdims=True)
    b = jnp.sum(lv[2:3] * lv[3:4], axis=1, keepdims=True)
    return jnp.exp(a) - jnp.exp(b) + lam_init


def _da_finish(acc0, l0, acc1, l1, lam, gain, lam_init):
    o = acc0 / l0 - lam * (acc1 / l1)
    y = o * lax.rsqrt(jnp.mean(o * o, axis=1, keepdims=True) + RMS_EPS)
    return (y * gain) * (1.0 - lam_init)


def _da_prompt_kernel(slope_ref, q_ref, k_ref, v_ref, lam_ref, g_ref, o_ref, m_ref, l_ref, acc_ref,
                      *, blk, hd, scale, lam_init):
    h = pl.program_id(1)
    qi = pl.program_id(2)
    slope = slope_ref[h]
    m_ref[...] = jnp.full(m_ref.shape, NEG_INF, F32)
    l_ref[...] = jnp.zeros(l_ref.shape, F32)
    acc_ref[...] = jnp.zeros(acc_ref.shape, F32)
    col = lax.broadcasted_iota(jnp.int32, (1, blk), 1)

    def step(j, bias, mask):
        st = pl.multiple_of(j * blk, blk)
        v = v_ref[pl.ds(st, blk), :]
        for m in range(2):
            sl = slice(m * hd, (m + 1) * hd)
            s = _dot_nt(q_ref[:, sl], k_ref[pl.ds(st, blk), sl]) * scale + bias
            if mask is not None:
                s = jnp.where(mask, s, NEG_INF)
            m_ref[m], l_ref[m], acc_ref[m] = _da_update(s, v, m_ref[m], l_ref[m], acc_ref[m])

    def body(j, c):
        step(j, slope * (col + (j - qi) * blk).astype(F32), None)
        return c

    lax.fori_loop(0, qi, body, 0)
    step(qi, _da_same_block_bias(blk, blk, slope), _da_chunk_mask(blk, blk))
    lam = _da_lambda(lam_ref, lam_init)
    o_ref[...] = _da_finish(acc_ref[0], l_ref[0], acc_ref[1], l_ref[1], lam, g_ref[...],
                            lam_init).astype(o_ref.dtype)


def _da_prompt(q, k, v, slopes, lam_vecs, gain, batch, seq, heads, hd, vd, lam_init):
    blk = min(256, seq)
    assert seq % blk == 0 and blk % CHUNK == 0
    nq = seq // blk
    blocks = blk * 2 * hd * 2 + blk * vd * 2 + seq * 2 * hd * 2 + seq * vd * 2
    scratch = 2 * 2 * blk * LANES * 4 + 2 * blk * vd * 4
    return pl.pallas_call(
        functools.partial(_da_prompt_kernel, blk=blk, hd=hd, scale=hd ** -0.5, lam_init=lam_init),
        grid=(batch, heads, nq),
        in_specs=[pl.BlockSpec(memory_space=pltpu.SMEM),
                  pl.BlockSpec((blk, 2 * hd), lambda b, h, i: (b * nq + i, h)),
                  pl.BlockSpec((seq, 2 * hd), lambda b, h, i: (b, h)),
                  pl.BlockSpec((seq, vd), lambda b, h, i: (b, h)),
                  pl.BlockSpec((4, hd), lambda b, h, i: (0, 0)),
                  pl.BlockSpec((1, vd), lambda b, h, i: (0, 0))],
        out_specs=pl.BlockSpec((blk, vd), lambda b, h, i: (b * nq + i, h)),
        out_shape=jax.ShapeDtypeStruct(v.shape, BF16),
        scratch_shapes=[pltpu.VMEM((2, blk, 1), F32), pltpu.VMEM((2, blk, 1), F32),
                        pltpu.VMEM((2, blk, vd), F32)],
        compiler_params=pltpu.CompilerParams(
            dimension_semantics=("arbitrary", "arbitrary", "arbitrary"),
            vmem_limit_bytes=_vmem_limit(blocks, scratch)),
        name="da_prompt",
    )(slopes, q, k, v, lam_vecs, gain)


def _da_sample_kernel(q_ref, kn_ref, vn_ref, kc_ref, vc_ref, lam_ref, g_ref, o_ref,
                      m_ref, l_ref, acc_ref, *, heads, hd, vd, past, slopes, scale, lam_init):
    kb = pl.program_id(1)
    nkb = pl.num_programs(1) - 1
    nq = q_ref.shape[0]
    bk = kc_ref.shape[0] // SUBLANES

    @pl.when(kb == 0)
    def _():
        m_ref[...] = jnp.full(m_ref.shape, NEG_INF, F32)
        l_ref[...] = jnp.zeros(l_ref.shape, F32)
        acc_ref[...] = jnp.zeros(acc_ref.shape, F32)

    def step(k_of, v_of, bias_of):
        for h in range(heads):
            v = v_of(h)
            bias = bias_of(slopes[h])
            vs = slice(h * vd, (h + 1) * vd)
            for m in range(2):
                i = 2 * h + m
                s = _dot_nt(q_ref[:, i * hd:(i + 1) * hd], k_of(i)) * scale + bias
                m_ref[i], l_ref[i], acc_ref[m, :, vs] = _da_update(
                    s, v, m_ref[i], l_ref[i], acc_ref[m, :, vs])

    @pl.when(kb < nkb)
    def _():
        col = lax.broadcasted_iota(jnp.int32, (1, bk), 1)
        rel = (col + (kb * bk - past)).astype(F32)
        step(lambda i: _tile_rows(kc_ref, i, bk).astype(BF16),
             lambda h: jnp.concatenate([_tile_rows(vc_ref, half * heads + h, bk) for half in range(vd // LANES)],
                                       axis=1).astype(BF16),
             lambda slope: slope * rel)

    @pl.when(kb == nkb)
    def _():
        step(lambda i: kn_ref[:, i * hd:(i + 1) * hd], lambda h: vn_ref[:, h * vd:(h + 1) * vd],
             lambda slope: _da_same_block_bias(nq, nq, slope))
        lam = _da_lambda(lam_ref, lam_init)
        for h in range(heads):
            vs = slice(h * vd, (h + 1) * vd)
            o_ref[:, vs] = _da_finish(acc_ref[0, :, vs], l_ref[2 * h], acc_ref[1, :, vs], l_ref[2 * h + 1],
                                      lam, g_ref[...], lam_init).astype(o_ref.dtype)


def _da_sample(q, kn, vn, kc, vc, slopes, lam_vecs, gain, batch, nq, past, heads, hd, vd, lam_init):
    assert nq == CHUNK and past % CHUNK == 0
    assert hd == LANES and 2 * heads == SUBLANES and heads * (vd // LANES) == SUBLANES
    qk_w = heads * 2 * hd
    v_w = heads * vd
    bk = min(512, past)
    assert past % bk == 0
    nkb = past // bk
    blocks = 2 * nq * qk_w * 2 + 2 * nq * v_w * 2 + bk * (qk_w + v_w) * 4
    scratch = 2 * 2 * heads * nq * LANES * 4 + 2 * nq * v_w * 4
    cache_idx = lambda b, j: (b * nkb + jnp.minimum(j, nkb - 1), 0)
    return pl.pallas_call(
        functools.partial(_da_sample_kernel, heads=heads, hd=hd, vd=vd, past=past, slopes=slopes,
                          scale=hd ** -0.5, lam_init=lam_init),
        grid=(batch, nkb + 1),
        in_specs=[pl.BlockSpec((nq, qk_w), lambda b, j: (b, 0)),
                  pl.BlockSpec((nq, qk_w), lambda b, j: (b, 0)),
                  pl.BlockSpec((nq, v_w), lambda b, j: (b, 0)),
                  pl.BlockSpec((bk * SUBLANES, LANES), cache_idx),
                  pl.BlockSpec((bk * SUBLANES, LANES), cache_idx),
                  pl.BlockSpec((4, hd), lambda b, j: (0, 0)),
                  pl.BlockSpec((1, vd), lambda b, j: (0, 0))],
        out_specs=pl.BlockSpec((nq, v_w), lambda b, j: (b, 0)),
        out_shape=jax.ShapeDtypeStruct(vn.shape, BF16),
        scratch_shapes=[pltpu.VMEM((2 * heads, nq, 1), F32), pltpu.VMEM((2 * heads, nq, 1), F32),
                        pltpu.VMEM((2, nq, v_w), F32)],
        compiler_params=pltpu.CompilerParams(
            dimension_semantics=("arbitrary", "arbitrary"),
            vmem_limit_bytes=_vmem_limit(blocks, scratch)),
        name="da_sample",
    )(q, kn, vn, kc, vc, lam_vecs, gain)


def _route(logits, n_groups, per_group):
    n_exp = n_groups * per_group
    lane = lax.broadcasted_iota(jnp.int32, logits.shape, 1).astype(F32)
    big = jnp.float32(3e38)

    def first_argmax(x):
        top = jnp.max(x, axis=1, keepdims=True)
        idx = jnp.min(jnp.where(x == top, lane, big), axis=1, keepdims=True)
        return top, idx

    is_coarse = (lane >= n_exp) & (lane < n_exp + n_groups)
    coarse = jnp.where(is_coarse, logits, -big)
    c_top, c_idx = first_argmax(coarse)
    g_gate = 1.0 / jnp.sum(jnp.where(is_coarse, jnp.exp(logits - c_top), 0.0), axis=1, keepdims=True)
    g_first = (c_idx - n_exp) * per_group
    in_group = (lane >= g_first) & (lane < g_first + per_group)
    fine = jnp.where(in_group, logits, -big)
    v1, i1 = first_argmax(fine)
    v2, i2 = first_argmax(jnp.where(lane == i1, -big, fine))
    e2 = jnp.exp(v2 - v1)
    w1 = g_gate / (1.0 + e2)
    w2 = g_gate * e2 / (1.0 + e2)
    return jnp.where(lane == i1, w1, jnp.where(lane == i2, w2, 0.0))


def _post_attn_kernel(sb_ref, da_ref, x_ref, wsb_ref, wda_ref, g_ref, b_ref, wr_ref, br_ref,
                      h1_ref, comb_ref, *, alpha, n_groups, per_group):
    mixed = _dot(sb_ref[...], wsb_ref[...]) + _dot(da_ref[...], wda_ref[...])
    h1 = _layer_norm(alpha * x_ref[...] + mixed, g_ref[...], b_ref[...])
    h1_ref[...] = h1
    logits = jnp.dot(h1, wr_ref[...], preferred_element_type=F32,
                     precision=lax.Precision.HIGHEST) + br_ref[...]
    comb_ref[...] = _route(logits, n_groups, per_group)


def _post_attn(sb_o, da_o, x, w_out, ln_g, ln_b, w_route, b_route, alpha, n_groups, per_group):
    t, d = x.shape
    w_sb = sb_o.shape[1]
    w_da = da_o.shape[1]
    assert w_sb == w_da
    tm = min(256, t)
    assert t % tm == 0
    blocks = (tm * (w_sb + w_da) * 2 + tm * d * 4 * 2 + (w_sb + w_da) * d * 2
              + d * LANES * 4 + tm * LANES * 4)
    row = lambda i: (i, 0)
    fixed = lambda i: (0, 0)
    return pl.pallas_call(
        functools.partial(_post_attn_kernel, alpha=alpha, n_groups=n_groups, per_group=per_group),
        grid=(t // tm,),
        in_specs=[pl.BlockSpec((tm, w_sb), row), pl.BlockSpec((tm, w_da), row), pl.BlockSpec((tm, d), row),
                  pl.BlockSpec((w_sb, d), fixed), pl.BlockSpec((w_da, d), lambda i: (1, 0)),
                  pl.BlockSpec((1, d), fixed), pl.BlockSpec((1, d), fixed),
                  pl.BlockSpec((d, LANES), fixed), pl.BlockSpec((1, LANES), fixed)],
        out_specs=[pl.BlockSpec((tm, d), row), pl.BlockSpec((tm, LANES), row)],
        out_shape=[jax.ShapeDtypeStruct((t, d), F32), jax.ShapeDtypeStruct((t, LANES), F32)],
        compiler_params=pltpu.CompilerParams(
            dimension_semantics=("arbitrary",), vmem_limit_bytes=_vmem_limit(blocks)),
        name="post_attn",
    )(sb_o, da_o, x, w_out, w_out, ln_g, ln_b, w_route, b_route)


def _moe_kernel(h1_ref, comb_ref, wg_ref, wu_ref, wd_ref, g_ref, b_ref, y_ref, xb_ref, acc_ref, *, alpha):
    e = pl.program_id(1)

    @pl.when(e == 0)
    def _():
        xb_ref[...] = h1_ref[...].astype(BF16)
        acc_ref[...] = jnp.zeros(acc_ref.shape, F32)

    xb = xb_ref[...]
    gate = _dot(xb, wg_ref[...])
    up = _dot(xb, wu_ref[...])
    hidden = (gate / (1.0 + jnp.exp(-gate))) * up
    lane = lax.broadcasted_iota(jnp.int32, comb_ref.shape, 1)
    weight = jnp.sum(jnp.where(lane == e, comb_ref[...], 0.0), axis=1, keepdims=True)
    acc_ref[...] += weight * _dot(hidden.astype(BF16), wd_ref[...])

    @pl.when(e == pl.num_programs(1) - 1)
    def _():
        y_ref[...] = _layer_norm(alpha * h1_ref[...] + acc_ref[...], g_ref[...], b_ref[...])


def _moe(h1, comb, w_gate, w_up, w_down, ln_g, ln_b, alpha):
    t, d = h1.shape
    n_exp, _, d_e = w_gate.shape
    tm = min(512, t)
    assert t % tm == 0
    blocks = 2 * tm * d * 4 + tm * LANES * 4 + 3 * d * d_e * 2
    scratch = tm * d * 2 + tm * d * 4
    row = lambda i, e: (i, 0)
    return pl.pallas_call(
        functools.partial(_moe_kernel, alpha=alpha),
        grid=(t // tm, n_exp),
        in_specs=[pl.BlockSpec((tm, d), row), pl.BlockSpec((tm, LANES), row),
                  pl.BlockSpec((None, d, d_e), lambda i, e: (e, 0, 0)),
                  pl.BlockSpec((None, d, d_e), lambda i, e: (e, 0, 0)),
                  pl.BlockSpec((None, d_e, d), lambda i, e: (e, 0, 0)),
                  pl.BlockSpec((1, d), lambda i, e: (0, 0)), pl.BlockSpec((1, d), lambda i, e: (0, 0))],
        out_specs=pl.BlockSpec((tm, d), row),
        out_shape=jax.ShapeDtypeStruct((t, d), F32),
        scratch_shapes=[pltpu.VMEM((tm, d), BF16), pltpu.VMEM((tm, d), F32)],
        compiler_params=pltpu.CompilerParams(
            dimension_semantics=("arbitrary", "arbitrary"),
            vmem_limit_bytes=_vmem_limit(blocks, scratch)),
        name="moe",
    )(h1, comb, w_gate, w_up, w_down, ln_g, ln_b)


def kernel(x_prompt, x_sample, cache_sb_k, cache_sb_v, cache_da_k, cache_da_v, w_in, w_out, lambda_q1, lambda_k1, lambda_q2, lambda_k2, subln_g, ln1_g, ln1_b, w_coarse, b_coarse, w_fine, b_fine, w_gate, w_up, w_down, ln2_g, ln2_b):
    depth, dec_batch, past, sb_heads, sb_hd = cache_sb_k.shape
    _, _, _, da_heads, _, da_hd = cache_da_k.shape
    da_vd = cache_da_v.shape[-1]
    batch, seq, d_model = x_prompt.shape
    dec_seq = x_sample.shape[1]
    n_groups, per_group = w_fine.shape[2], w_fine.shape[3]
    n_exp = n_groups * per_group
    assert n_exp + n_groups <= LANES
    sb_w = sb_heads * sb_hd
    da_qk_w = da_heads * 2 * da_hd
    da_w = da_heads * da_vd
    assert sb_w == da_qk_w == da_w, "projection column groups are addressed as equal-width blocks"
    alpha = (2 * depth) ** 0.25
    slopes = tuple(2.0 ** (-8.0 * (h + 1) / da_heads) for h in range(da_heads))
    slopes_arr = jnp.asarray(slopes, F32)

    hp = x_prompt.reshape(batch * seq, d_model)
    hs = x_sample.reshape(dec_batch * dec_seq, d_model)
    rows_p, rows_s = [], []
    for l in range(depth):
        lam_init = 0.8 - 0.6 * math.exp(-0.3 * l)
        w_in_b = w_in[l].astype(BF16)
        w_out_b = w_out[l].astype(BF16)
        lam_vecs = jnp.stack([lambda_q1[l], lambda_k1[l], lambda_q2[l], lambda_k2[l]]).astype(F32)
        gain = subln_g[l].reshape(1, da_vd)
        w_route = jnp.concatenate(
            [w_fine[l].reshape(d_model, n_exp), w_coarse[l],
             jnp.zeros((d_model, LANES - n_exp - n_groups), F32)], axis=1)
        b_route = jnp.concatenate(
            [b_fine[l].reshape(n_exp), b_coarse[l], jnp.zeros((LANES - n_exp - n_groups,), F32)]).reshape(1, LANES)
        wg_b, wu_b, wd_b = w_gate[l].astype(BF16), w_up[l].astype(BF16), w_down[l].astype(BF16)
        g1, b1 = ln1_g[l].reshape(1, d_model), ln1_b[l].reshape(1, d_model)
        g2, b2 = ln2_g[l].reshape(1, d_model), ln2_b[l].reshape(1, d_model)

        halves = da_vd // LANES
        dv_slots = tuple(h * halves + half for half in range(halves) for h in range(da_heads))
        plain_slots = tuple(range(SUBLANES))

        def project(x, tag):
            (sq,) = _proj(x, w_in_b, 0, None, "proj_sq_" + tag)
            sk_b, sk = _proj(x, w_in_b, 1, plain_slots, "proj_sk_" + tag)
            sv_b, sv = _proj(x, w_in_b, 2, plain_slots, "proj_sv_" + tag)
            (dq,) = _proj(x, w_in_b, 3, None, "proj_dq_" + tag)
            dk_b, dk = _proj(x, w_in_b, 4, plain_slots, "proj_dk_" + tag)
            dv_b, dv = _proj(x, w_in_b, 5, dv_slots, "proj_dv_" + tag)
            return (sq, sk_b, sv_b, dq, dk_b, dv_b), (sk, sv, dk, dv)

        def dv_tiles_to_rows(tiles, lead):
            return tiles.reshape(lead + (halves, da_heads, LANES)).swapaxes(-3, -2).reshape(lead + (da_heads, da_vd))

        def dv_rows_to_tiles(rows_, n):
            return rows_.reshape(n, da_heads, halves, LANES).swapaxes(1, 2).reshape(n * SUBLANES, LANES)

        (sq, sk, sv, dq, dk, dv), new_p = project(hp, "p")
        sb_o = _sb_prompt(sq, sk, sv, batch, seq, sb_heads, sb_hd)
        da_o = _da_prompt(dq, dk, dv, slopes_arr, lam_vecs, gain, batch, seq, da_heads, da_hd, da_vd, lam_init)
        h1, comb = _post_attn(sb_o, da_o, hp, w_out_b, g1, b1, w_route, b_route, alpha, n_groups, per_group)
        hp = _moe(h1, comb, wg_b, wu_b, wd_b, g2, b2, alpha)

        (sq, sk, sv, dq, dk, dv), new_s = project(hs, "s")
        rows = dec_batch * past
        sb_o = _sb_sample(sq, sk, sv, cache_sb_k[l].reshape(rows * SUBLANES, LANES),
                          cache_sb_v[l].reshape(rows * SUBLANES, LANES),
                          dec_batch, dec_seq, past, sb_heads, sb_hd)
        da_o = _da_sample(dq, dk, dv, cache_da_k[l].reshape(rows * SUBLANES, LANES),
                          dv_rows_to_tiles(cache_da_v[l], rows),
                          slopes, lam_vecs, gain, dec_batch, dec_seq, past, da_heads, da_hd, da_vd, lam_init)
        h1, comb = _post_attn(sb_o, da_o, hs, w_out_b, g1, b1, w_route, b_route, alpha, n_groups, per_group)
        hs = _moe(h1, comb, wg_b, wu_b, wd_b, g2, b2, alpha)
        for new, lead, rows_out in ((new_p, (batch, seq), rows_p), (new_s, (dec_batch, dec_seq), rows_s)):
            rows_out.append((new[0].reshape(lead + (sb_heads, sb_hd)), new[1].reshape(lead + (sb_heads, sb_hd)),
                             new[2].reshape(lead + (da_heads, 2, da_hd)), dv_tiles_to_rows(new[3], lead)))

    def stack(rows, i):
        return rows[0][i][None] if len(rows) == 1 else jnp.stack([r[i] for r in rows], axis=0)

    return (hp.reshape(batch, seq, d_model), hs.reshape(dec_batch, dec_seq, d_model),
            *(stack(rows_p, i) for i in range(4)), *(stack(rows_s, i) for i in range(4)))
```

```python
import functools
import math

import jax
import jax.numpy as jnp
from jax import lax
from jax.experimental import pallas as pl
from jax.experimental.pallas import tpu as pltpu

F32 = jnp.float32
BF16 = jnp.bfloat16

CHUNK = 64
LN_EPS = 1e-5
RMS_EPS = 1e-5
NEG_INF = -1e30
LOG2E = 1.4426950408889634
LANES = 128
SUBLANES = 8
VMEM_CAP = 64 * 2**20
SB_SKIP_LOG2 = 150.0


def _vmem_limit(block_bytes, scratch_bytes=0):
    est = 2 * block_bytes + scratch_bytes + 16 * 2**20
    return int(min(est, VMEM_CAP - 6 * 2**20))


def _dot(a, b):
    return jnp.dot(a, b, preferred_element_type=F32)


def _dot_nt(a, b):
    return lax.dot_general(a, b, (((1,), (1,)), ((), ())), preferred_element_type=F32)


def _widen(x, width):
    if width < LANES:
        return x[:, :width]
    return x if width == LANES else pltpu.repeat(x, width // LANES, axis=1)


def _split_bf16(x):
    hi = x.astype(BF16)
    return hi, (x - hi.astype(F32)).astype(BF16)


def _layer_norm(y, g, b):
    mu = jnp.mean(y, axis=1, keepdims=True)
    yc = y - mu
    var = jnp.mean(yc * yc, axis=1, keepdims=True)
    return yc * lax.rsqrt(var + LN_EPS) * g + b


def _tile_rows(ref, slot, n):
    return ref[pl.ds(slot, n, stride=SUBLANES), :]


def _proj_kernel(x_ref, w_ref, o16_ref, *o32_refs, slots):
    acc = _dot(x_ref[...].astype(BF16), w_ref[...])
    o16_ref[...] = acc.astype(BF16)
    for o32_ref in o32_refs:
        for j, cb in enumerate(slots):
            o32_ref[pl.ds(j, acc.shape[0], stride=SUBLANES), :] = acc[:, cb * LANES:(cb + 1) * LANES]


def _proj(x, w, col_block, slots, name):
    t, d = x.shape
    width = SUBLANES * LANES
    tm = min(512, t)
    assert t % tm == 0
    blk = tm * d * 4 + d * width * 2 + tm * width * (2 + (4 if slots else 0))
    out_specs = [pl.BlockSpec((tm, width), lambda i: (i, 0))]
    out_shape = [jax.ShapeDtypeStruct((t, width), BF16)]
    if slots:
        out_specs.append(pl.BlockSpec((tm * SUBLANES, LANES), lambda i: (i, 0)))
        out_shape.append(jax.ShapeDtypeStruct((t * SUBLANES, LANES), F32))
    return pl.pallas_call(
        functools.partial(_proj_kernel, slots=slots),
        grid=(t // tm,),
        in_specs=[pl.BlockSpec((tm, d), lambda i: (i, 0)),
                  pl.BlockSpec((d, width), lambda i: (0, col_block))],
        out_specs=out_specs,
        out_shape=out_shape,
        compiler_params=pltpu.CompilerParams(
            dimension_semantics=("arbitrary",), vmem_limit_bytes=_vmem_limit(blk)),
        name=name,
    )(x, w)


def _inclusive_suffix_matrix(n):
    r = lax.broadcasted_iota(jnp.int32, (2 * n, n), 0)
    c = lax.broadcasted_iota(jnp.int32, (2 * n, n), 1)
    return (jnp.where(r >= n, r - n, r) >= c).astype(BF16)


def _strict_causal_mask(nq, nk):
    r = lax.broadcasted_iota(jnp.int32, (nq, nk), 0)
    c = lax.broadcasted_iota(jnp.int32, (nq, nk), 1)
    return c < r


def _sb_blocks(qs, ks, vs, carries, accs, umat, mask, scale2):
    n = len(qs)
    nq = qs[0].shape[0]
    zs = [_dot_nt(qs[i], ks[i]) * scale2 for i in range(n)]
    splits = []
    for z in zs:
        neg_abs = lax.bitcast_convert_type(
            lax.bitcast_convert_type(z, jnp.uint32) | jnp.uint32(0x80000000), F32)
        sp = jnp.maximum(z, 0.0) + jnp.log(1.0 + jnp.exp2(neg_abs)) * LOG2E
        if mask is not None:
            sp = jnp.where(mask, sp, 0.0)
        splits.append(jnp.concatenate(_split_bf16(sp), axis=1))
    incl_all = _dot(jnp.concatenate(splits, axis=0) if n > 1 else splits[0], umat)
    out_c, out_a = [], []
    for i in range(n):
        incl = incl_all[i * nq:(i + 1) * nq]
        w = jnp.exp2(zs[i] - incl - carries[i])
        if mask is not None:
            w = jnp.where(mask, w, 0.0)
        out_a.append(accs[i] + _dot(w.astype(BF16), vs[i]))
        out_c.append(carries[i] + incl[:, :1])
    return out_c, out_a


def _all_exhausted(carries):
    lowest = functools.reduce(jnp.minimum, carries)
    return jnp.min(lowest) >= SB_SKIP_LOG2


def _sb_prompt_kernel(q_ref, k_ref, v_ref, o_ref, carry_ref, acc_ref, *, blk, hd, scale2):
    qi = pl.program_id(2)
    nh = q_ref.shape[1] // hd
    umat = _inclusive_suffix_matrix(blk)
    sls = [slice(h * hd, (h + 1) * hd) for h in range(nh)]

    def run(start, carries, accs, mask):
        cs, as_ = _sb_blocks([q_ref[:, sl] for sl in sls], [k_ref[pl.ds(start, blk), sl] for sl in sls],
                             [v_ref[pl.ds(start, blk), sl] for sl in sls], carries, accs, umat, mask, scale2)
        for h in range(nh):
            carry_ref[h] = cs[h]
            acc_ref[:, sls[h]] = as_[h]
        return cs

    run(pl.multiple_of(qi * blk, blk), [jnp.zeros((blk, 1), F32)] * nh, [jnp.zeros((blk, hd), F32)] * nh,
        _strict_causal_mask(blk, blk))

    def cond(state):
        j, go = state
        return jnp.logical_and(j >= 0, go > 0)

    def body(state):
        j, _ = state
        cs = run(pl.multiple_of(j * blk, blk), [carry_ref[h] for h in range(nh)],
                 [acc_ref[:, sl] for sl in sls], None)
        return j - 1, jnp.logical_not(_all_exhausted(cs)).astype(jnp.int32)

    lax.while_loop(cond, body, (qi - 1, jnp.int32(1)))
    o_ref[...] = acc_ref[...].astype(o_ref.dtype)


def _sb_prompt(q, k, v, batch, seq, heads, hd):
    blk = min(256, seq)
    hb = math.gcd(heads, 4)
    assert seq % blk == 0
    nq = seq // blk
    w = hb * hd
    blocks = blk * w * 2 * 2 + 2 * seq * w * 2
    scratch = hb * blk * LANES * 4 + blk * w * 4
    return pl.pallas_call(
        functools.partial(_sb_prompt_kernel, blk=blk, hd=hd, scale2=hd ** -0.5 * LOG2E),
        grid=(batch, heads // hb, nq),
        in_specs=[pl.BlockSpec((blk, w), lambda b, h, i: (b * nq + i, h)),
                  pl.BlockSpec((seq, w), lambda b, h, i: (b, h)),
                  pl.BlockSpec((seq, w), lambda b, h, i: (b, h))],
        out_specs=pl.BlockSpec((blk, w), lambda b, h, i: (b * nq + i, h)),
        out_shape=jax.ShapeDtypeStruct(q.shape, BF16),
        scratch_shapes=[pltpu.VMEM((hb, blk, 1), F32), pltpu.VMEM((blk, w), F32)],
        compiler_params=pltpu.CompilerParams(
            dimension_semantics=("arbitrary", "arbitrary", "arbitrary"),
            vmem_limit_bytes=_vmem_limit(blocks, scratch)),
        name="sb_prompt",
    )(q, k, v)


def _sb_sample_kernel(q_ref, kn_ref, vn_ref, kc_ref, vc_ref, o_ref, carry_ref, acc_ref, u_ref, go_ref,
                      *, heads, hd, scale2):
    kb = pl.program_id(1)
    nq = q_ref.shape[0]
    bk = u_ref.shape[1]
    sls = [slice(h * hd, (h + 1) * hd) for h in range(heads)]

    def run(ks, vs, carries, accs, umat, mask):
        cs, as_ = _sb_blocks([q_ref[:, sl] for sl in sls], ks, vs, carries, accs, umat, mask, scale2)
        for h in range(heads):
            carry_ref[h] = cs[h]
            acc_ref[:, sls[h]] = as_[h]
        go_ref[0] = jnp.logical_not(_all_exhausted(cs)).astype(jnp.int32)

    @pl.when(kb == 0)
    def _():
        u_ref[...] = _inclusive_suffix_matrix(bk)
        run([kn_ref[:, sl] for sl in sls], [vn_ref[:, sl] for sl in sls],
            [jnp.zeros((nq, 1), F32)] * heads, [jnp.zeros((nq, hd), F32)] * heads,
            _inclusive_suffix_matrix(nq), _strict_causal_mask(nq, nq))

    @pl.when(jnp.logical_and(kb > 0, go_ref[0] > 0))
    def _():
        run([_tile_rows(kc_ref, h, bk).astype(BF16) for h in range(heads)],
            [_tile_rows(vc_ref, h, bk).astype(BF16) for h in range(heads)],
            [carry_ref[h] for h in range(heads)], [acc_ref[:, sl] for sl in sls], u_ref[...], None)

    @pl.when(kb == pl.num_programs(1) - 1)
    def _():
        o_ref[...] = acc_ref[...].astype(o_ref.dtype)


def _sb_sample(q, kn, vn, kc, vc, batch, nq, past, heads, hd):
    assert heads == SUBLANES and hd == LANES
    width = heads * hd
    bk = min(512, past)
    assert past % bk == 0
    nkb = past // bk
    new_spec = pl.BlockSpec((nq, width), lambda b, j: (b, 0))
    cache_spec = pl.BlockSpec((bk * SUBLANES, LANES), lambda b, j: (b * nkb + nkb - jnp.maximum(j, 1), 0))
    blocks = 4 * nq * width * 2 + 2 * bk * width * 4
    scratch = heads * nq * LANES * 4 + nq * width * 4 + 2 * bk * bk * 2
    return pl.pallas_call(
        functools.partial(_sb_sample_kernel, heads=heads, hd=hd, scale2=hd ** -0.5 * LOG2E),
        grid=(batch, nkb + 1),
        in_specs=[new_spec, new_spec, new_spec, cache_spec, cache_spec],
        out_specs=new_spec,
        out_shape=jax.ShapeDtypeStruct(q.shape, BF16),
        scratch_shapes=[pltpu.VMEM((heads, nq, 1), F32), pltpu.VMEM((nq, width), F32),
                        pltpu.VMEM((2 * bk, bk), BF16), pltpu.SMEM((1,), jnp.int32)],
        compiler_params=pltpu.CompilerParams(
            dimension_semantics=("arbitrary", "arbitrary"),
            vmem_limit_bytes=_vmem_limit(blocks, scratch)),
        name="sb_sample",
    )(q, kn, vn, kc, vc)


def _da_steps(ss, vs, m_refs, l_refs, acc_refs):
    n = len(ss)
    nk = ss[0].shape[1]
    m_olds = [r[...] for r in m_refs]
    m_news = [jnp.maximum(m_olds[i], jnp.max(ss[i], axis=1, keepdims=True)) for i in range(n)]
    ps = [jnp.exp2(ss[i] - _widen(m_news[i], nk)) for i in range(n)]
    for i in range(n):
        alpha = jnp.exp2(m_olds[i] - m_news[i])
        l_refs[i][...] = alpha * l_refs[i][...] + jnp.sum(ps[i], axis=1, keepdims=True)
        acc_refs[i][...] = (_widen(alpha, vs[i].shape[1]) * acc_refs[i][...]
                            + _dot(ps[i].astype(BF16), vs[i]))
        m_refs[i][...] = m_news[i]


def _da_same_block_bias(nq, nk, slope2):
    r = lax.broadcasted_iota(jnp.int32, (nq, nk), 0)
    c = lax.broadcasted_iota(jnp.int32, (nq, nk), 1)
    return slope2 * (r - jnp.abs(r - c)).astype(F32)


def _da_chunk_mask(nq, nk):
    r = lax.broadcasted_iota(jnp.int32, (nq, nk), 0)
    c = lax.broadcasted_iota(jnp.int32, (nq, nk), 1)
    return (c // CHUNK) <= (r // CHUNK)


def _da_lambda(lam_ref, lam_init):
    lv = lam_ref[...]
    a = jnp.sum(lv[0:1] * lv[1:2], axis=1, keepdims=True)
    b = jnp.sum(lv[2:3] * lv[3:4], axis=1, keepdims=True)
    return jnp.exp(a) - jnp.exp(b) + lam_init


def _da_finish(acc0, l0, acc1, l1, lam, gain, lam_init):
    vd = acc0.shape[1]
    o = acc0 * _widen(1.0 / l0, vd) - lam * (acc1 * _widen(1.0 / l1, vd))
    y = o * lax.rsqrt(jnp.mean(o * o, axis=1, keepdims=True) + RMS_EPS)
    return (y * gain) * (1.0 - lam_init)


def _da_prompt_kernel(slope_ref, q_ref, k_ref, v_ref, lam_ref, g_ref, o_ref, m_ref, l_ref, acc_ref,
                      *, blk, hd, scale2, lam_init):
    h = pl.program_id(1)
    qi = pl.program_id(2)
    slope2 = slope_ref[h] * LOG2E
    m_ref[...] = jnp.full(m_ref.shape, NEG_INF, F32)
    l_ref[...] = jnp.zeros(l_ref.shape, F32)
    acc_ref[...] = jnp.zeros(acc_ref.shape, F32)
    col = lax.broadcasted_iota(jnp.int32, (1, blk), 1)
    maps = lambda ref: [ref.at[0], ref.at[1]]

    def step(j, bias, mask):
        st = pl.multiple_of(j * blk, blk)
        ss = []
        for m in range(2):
            sl = slice(m * hd, (m + 1) * hd)
            s = _dot_nt(q_ref[:, sl], k_ref[pl.ds(st, blk), sl]) * scale2 + bias
            ss.append(s if mask is None else jnp.where(mask, s, NEG_INF))
        v = v_ref[pl.ds(st, blk), :]
        _da_steps(ss, [v, v], maps(m_ref), maps(l_ref), maps(acc_ref))

    def body(j, c):
        step(j, slope2 * (col + (j - qi) * blk).astype(F32), None)
        return c

    lax.fori_loop(0, qi, body, 0)
    step(qi, _da_same_block_bias(blk, blk, slope2), _da_chunk_mask(blk, blk))
    lam = _da_lambda(lam_ref, lam_init)
    o_ref[...] = _da_finish(acc_ref[0], l_ref[0], acc_ref[1], l_ref[1], lam, g_ref[...],
                            lam_init).astype(o_ref.dtype)


def _da_prompt(q, k, v, slopes, lam_vecs, gain, batch, seq, heads, hd, vd, lam_init):
    blk = min(512, seq)
    assert seq % blk == 0 and blk % CHUNK == 0
    nq = seq // blk
    blocks = blk * 2 * hd * 2 + blk * vd * 2 + seq * 2 * hd * 2 + seq * vd * 2
    scratch = 2 * 2 * blk * LANES * 4 + 2 * blk * vd * 4
    return pl.pallas_call(
        functools.partial(_da_prompt_kernel, blk=blk, hd=hd, scale2=hd ** -0.5 * LOG2E, lam_init=lam_init),
        grid=(batch, heads, nq),
        in_specs=[pl.BlockSpec(memory_space=pltpu.SMEM),
                  pl.BlockSpec((blk, 2 * hd), lambda b, h, i: (b * nq + i, h)),
                  pl.BlockSpec((seq, 2 * hd), lambda b, h, i: (b, h)),
                  pl.BlockSpec((seq, vd), lambda b, h, i: (b, h)),
                  pl.BlockSpec((4, hd), lambda b, h, i: (0, 0)),
                  pl.BlockSpec((1, vd), lambda b, h, i: (0, 0))],
        out_specs=pl.BlockSpec((blk, vd), lambda b, h, i: (b * nq + i, h)),
        out_shape=jax.ShapeDtypeStruct(v.shape, BF16),
        scratch_shapes=[pltpu.VMEM((2, blk, LANES), F32), pltpu.VMEM((2, blk, LANES), F32),
                        pltpu.VMEM((2, blk, vd), F32)],
        compiler_params=pltpu.CompilerParams(
            dimension_semantics=("arbitrary", "arbitrary", "arbitrary"),
            vmem_limit_bytes=_vmem_limit(blocks, scratch)),
        name="da_prompt",
    )(slopes, q, k, v, lam_vecs, gain)


def _da_sample_kernel(q_ref, kn_ref, vn_ref, kc_ref, vc_ref, lam_ref, g_ref, o_ref,
                      m_ref, l_ref, acc_ref, *, heads, hd, vd, past, slopes, scale2, lam_init):
    kb = pl.program_id(1)
    nkb = pl.num_programs(1) - 1
    nq = q_ref.shape[0]
    bk = kc_ref.shape[0] // SUBLANES
    chains = 2 * heads
    refs = lambda ref: [ref.at[i] for i in range(chains)]

    @pl.when(kb == 0)
    def _():
        m_ref[...] = jnp.full(m_ref.shape, NEG_INF, F32)
        l_ref[...] = jnp.zeros(l_ref.shape, F32)
        acc_ref[...] = jnp.zeros(acc_ref.shape, F32)

    def step(k_of, v_of, bias_of):
        ss, vs = [], []
        for h in range(heads):
            v = v_of(h)
            bias = bias_of(slopes[h] * LOG2E)
            for m in range(2):
                i = 2 * h + m
                ss.append(_dot_nt(q_ref[:, i * hd:(i + 1) * hd], k_of(i)) * scale2 + bias)
                vs.append(v)
        _da_steps(ss, vs, refs(m_ref), refs(l_ref), refs(acc_ref))

    @pl.when(kb < nkb)
    def _():
        col = lax.broadcasted_iota(jnp.int32, (1, bk), 1)
        rel = (col + (kb * bk - past)).astype(F32)
        step(lambda i: _tile_rows(kc_ref, i, bk).astype(BF16),
             lambda h: jnp.concatenate([_tile_rows(vc_ref, half * heads + h, bk) for half in range(vd // LANES)],
                                       axis=1).astype(BF16),
             lambda slope2: slope2 * rel)

    @pl.when(kb == nkb)
    def _():
        step(lambda i: kn_ref[:, i * hd:(i + 1) * hd], lambda h: vn_ref[:, h * vd:(h + 1) * vd],
             lambda slope2: _da_same_block_bias(nq, nq, slope2))
        lam = _da_lambda(lam_ref, lam_init)
        for h in range(heads):
            o_ref[:, h * vd:(h + 1) * vd] = _da_finish(
                acc_ref[2 * h], l_ref[2 * h], acc_ref[2 * h + 1], l_ref[2 * h + 1],
                lam, g_ref[...], lam_init).astype(o_ref.dtype)


def _da_sample(q, kn, vn, kc, vc, slopes, lam_vecs, gain, batch, nq, past, heads, hd, vd, lam_init):
    assert nq == CHUNK and past % CHUNK == 0
    assert hd == LANES and 2 * heads == SUBLANES and heads * (vd // LANES) == SUBLANES
    qk_w = heads * 2 * hd
    v_w = heads * vd
    bk = min(512, past)
    assert past % bk == 0
    nkb = past // bk
    blocks = 2 * nq * qk_w * 2 + 2 * nq * v_w * 2 + bk * (qk_w + v_w) * 4
    scratch = 2 * 2 * heads * nq * LANES * 4 + 2 * heads * nq * vd * 4
    cache_idx = lambda b, j: (b * nkb + jnp.minimum(j, nkb - 1), 0)
    return pl.pallas_call(
        functools.partial(_da_sample_kernel, heads=heads, hd=hd, vd=vd, past=past, slopes=slopes,
                          scale2=hd ** -0.5 * LOG2E, lam_init=lam_init),
        grid=(batch, nkb + 1),
        in_specs=[pl.BlockSpec((nq, qk_w), lambda b, j: (b, 0)),
                  pl.BlockSpec((nq, qk_w), lambda b, j: (b, 0)),
                  pl.BlockSpec((nq, v_w), lambda b, j: (b, 0)),
                  pl.BlockSpec((bk * SUBLANES, LANES), cache_idx),
                  pl.BlockSpec((bk * SUBLANES, LANES), cache_idx),
                  pl.BlockSpec((4, hd), lambda b, j: (0, 0)),
                  pl.BlockSpec((1, vd), lambda b, j: (0, 0))],
        out_specs=pl.BlockSpec((nq, v_w), lambda b, j: (b, 0)),
        out_shape=jax.ShapeDtypeStruct(vn.shape, BF16),
        scratch_shapes=[pltpu.VMEM((2 * heads, nq, LANES), F32), pltpu.VMEM((2 * heads, nq, LANES), F32),
                        pltpu.VMEM((2 * heads, nq, vd), F32)],
        compiler_params=pltpu.CompilerParams(
            dimension_semantics=("arbitrary", "arbitrary"),
            vmem_limit_bytes=_vmem_limit(blocks, scratch)),
        name="da_sample",
    )(q, kn, vn, kc, vc, lam_vecs, gain)


def _route(logits, n_groups, per_group):
    n_exp = n_groups * per_group
    lane = lax.broadcasted_iota(jnp.int32, logits.shape, 1).astype(F32)
    big = jnp.float32(3e38)

    def first_argmax(x):
        top = jnp.max(x, axis=1, keepdims=True)
        idx = jnp.min(jnp.where(x == top, lane, big), axis=1, keepdims=True)
        return top, idx

    is_coarse = (lane >= n_exp) & (lane < n_exp + n_groups)
    coarse = jnp.where(is_coarse, logits, -big)
    c_top, c_idx = first_argmax(coarse)
    g_gate = 1.0 / jnp.sum(jnp.where(is_coarse, jnp.exp(logits - c_top), 0.0), axis=1, keepdims=True)
    g_first = (c_idx - n_exp) * per_group
    in_group = (lane >= g_first) & (lane < g_first + per_group)
    fine = jnp.where(in_group, logits, -big)
    v1, i1 = first_argmax(fine)
    v2, i2 = first_argmax(jnp.where(lane == i1, -big, fine))
    e2 = jnp.exp(v2 - v1)
    w1 = g_gate / (1.0 + e2)
    w2 = g_gate * e2 / (1.0 + e2)
    return jnp.where(lane == i1, w1, jnp.where(lane == i2, w2, 0.0))


def _post_attn_kernel(sb_ref, da_ref, x_ref, wsb_ref, wda_ref, g_ref, b_ref, wr_ref, br_ref,
                      h1_ref, comb_ref, *, alpha, n_groups, per_group):
    mixed = _dot(sb_ref[...], wsb_ref[...]) + _dot(da_ref[...], wda_ref[...])
    h1 = _layer_norm(alpha * x_ref[...] + mixed, g_ref[...], b_ref[...])
    h1_ref[...] = h1
    h_hi, h_lo = _split_bf16(h1)
    both = _dot(h_hi, wr_ref[...])
    logits = both[:, :LANES] + both[:, LANES:] + _dot(h_lo, wr_ref[:, :LANES]) + br_ref[...]
    comb_ref[...] = _route(logits, n_groups, per_group)


def _post_attn(sb_o, da_o, x, w_out, ln_g, ln_b, w_route, b_route, alpha, n_groups, per_group):
    t, d = x.shape
    w_sb = sb_o.shape[1]
    w_da = da_o.shape[1]
    assert w_sb == w_da
    tm = min(256, t)
    assert t % tm == 0
    blocks = (tm * (w_sb + w_da) * 2 + tm * d * 4 * 2 + (w_sb + w_da) * d * 2
              + d * 2 * LANES * 2 + tm * LANES * 4)
    row = lambda i: (i, 0)
    fixed = lambda i: (0, 0)
    return pl.pallas_call(
        functools.partial(_post_attn_kernel, alpha=alpha, n_groups=n_groups, per_group=per_group),
        grid=(t // tm,),
        in_specs=[pl.BlockSpec((tm, w_sb), row), pl.BlockSpec((tm, w_da), row), pl.BlockSpec((tm, d), row),
                  pl.BlockSpec((w_sb, d), fixed), pl.BlockSpec((w_da, d), lambda i: (1, 0)),
                  pl.BlockSpec((1, d), fixed), pl.BlockSpec((1, d), fixed),
                  pl.BlockSpec((d, 2 * LANES), fixed), pl.BlockSpec((1, LANES), fixed)],
        out_specs=[pl.BlockSpec((tm, d), row), pl.BlockSpec((tm, LANES), row)],
        out_shape=[jax.ShapeDtypeStruct((t, d), F32), jax.ShapeDtypeStruct((t, LANES), F32)],
        compiler_params=pltpu.CompilerParams(
            dimension_semantics=("arbitrary",), vmem_limit_bytes=_vmem_limit(blocks)),
        name="post_attn",
    )(sb_o, da_o, x, w_out, w_out, ln_g, ln_b, w_route, b_route)


def _moe_kernel(h1_ref, comb_ref, wg_ref, wu_ref, wd_ref, g_ref, b_ref, y_ref, xb_ref, acc_ref, *, alpha):
    e = pl.program_id(1)

    @pl.when(e == 0)
    def _():
        xb_ref[...] = h1_ref[...].astype(BF16)
        acc_ref[...] = jnp.zeros(acc_ref.shape, F32)

    xb = xb_ref[...]
    gate = _dot(xb, wg_ref[...])
    up = _dot(xb, wu_ref[...])
    hidden = (gate / (1.0 + jnp.exp(-gate))) * up
    lane = lax.broadcasted_iota(jnp.int32, comb_ref.shape, 1)
    weight = jnp.sum(jnp.where(lane == e, comb_ref[...], 0.0), axis=1, keepdims=True)
    acc_ref[...] += weight * _dot(hidden.astype(BF16), wd_ref[...])

    @pl.when(e == pl.num_programs(1) - 1)
    def _():
        y_ref[...] = _layer_norm(alpha * h1_ref[...] + acc_ref[...], g_ref[...], b_ref[...])


def _moe(h1, comb, w_gate, w_up, w_down, ln_g, ln_b, alpha):
    t, d = h1.shape
    n_exp, _, d_e = w_gate.shape
    tm = min(512, t)
    assert t % tm == 0
    blocks = 2 * tm * d * 4 + tm * LANES * 4 + 3 * d * d_e * 2
    scratch = tm * d * 2 + tm * d * 4
    row = lambda i, e: (i, 0)
    return pl.pallas_call(
        functools.partial(_moe_kernel, alpha=alpha),
        grid=(t // tm, n_exp),
        in_specs=[pl.BlockSpec((tm, d), row), pl.BlockSpec((tm, LANES), row),
                  pl.BlockSpec((None, d, d_e), lambda i, e: (e, 0, 0)),
                  pl.BlockSpec((None, d, d_e), lambda i, e: (e, 0, 0)),
                  pl.BlockSpec((None, d_e, d), lambda i, e: (e, 0, 0)),
                  pl.BlockSpec((1, d), lambda i, e: (0, 0)), pl.BlockSpec((1, d), lambda i, e: (0, 0))],
        out_specs=pl.BlockSpec((tm, d), row),
        out_shape=jax.ShapeDtypeStruct((t, d), F32),
        scratch_shapes=[pltpu.VMEM((tm, d), BF16), pltpu.VMEM((tm, d), F32)],
        compiler_params=pltpu.CompilerParams(
            dimension_semantics=("arbitrary", "arbitrary"),
            vmem_limit_bytes=_vmem_limit(blocks, scratch)),
        name="moe",
    )(h1, comb, w_gate, w_up, w_down, ln_g, ln_b)


def kernel(x_prompt, x_sample, cache_sb_k, cache_sb_v, cache_da_k, cache_da_v, w_in, w_out, lambda_q1, lambda_k1, lambda_q2, lambda_k2, subln_g, ln1_g, ln1_b, w_coarse, b_coarse, w_fine, b_fine, w_gate, w_up, w_down, ln2_g, ln2_b):
    depth, dec_batch, past, sb_heads, sb_hd = cache_sb_k.shape
    _, _, _, da_heads, _, da_hd = cache_da_k.shape
    da_vd = cache_da_v.shape[-1]
    batch, seq, d_model = x_prompt.shape
    dec_seq = x_sample.shape[1]
    n_groups, per_group = w_fine.shape[2], w_fine.shape[3]
    n_exp = n_groups * per_group
    assert n_exp + n_groups <= LANES
    sb_w = sb_heads * sb_hd
    assert sb_w == da_heads * 2 * da_hd == da_heads * da_vd == SUBLANES * LANES, \
        "projection column groups are addressed as equal-width blocks of one token tile"
    alpha = (2 * depth) ** 0.25
    slopes = tuple(2.0 ** (-8.0 * (h + 1) / da_heads) for h in range(da_heads))
    slopes_arr = jnp.asarray(slopes, F32)

    hp = x_prompt.reshape(batch * seq, d_model)
    hs = x_sample.reshape(dec_batch * dec_seq, d_model)
    rows_p, rows_s = [], []
    for l in range(depth):
        lam_init = 0.8 - 0.6 * math.exp(-0.3 * l)
        w_in_b = w_in[l].astype(BF16)
        w_out_b = w_out[l].astype(BF16)
        lam_vecs = jnp.stack([lambda_q1[l], lambda_k1[l], lambda_q2[l], lambda_k2[l]]).astype(F32)
        gain = subln_g[l].reshape(1, da_vd)
        w_route = jnp.concatenate(
            [w_fine[l].reshape(d_model, n_exp), w_coarse[l],
             jnp.zeros((d_model, LANES - n_exp - n_groups), F32)], axis=1)
        w_route_b = jnp.concatenate(_split_bf16(w_route), axis=1)
        b_route = jnp.concatenate(
            [b_fine[l].reshape(n_exp), b_coarse[l], jnp.zeros((LANES - n_exp - n_groups,), F32)]).reshape(1, LANES)
        wg_b, wu_b, wd_b = w_gate[l].astype(BF16), w_up[l].astype(BF16), w_down[l].astype(BF16)
        g1, b1 = ln1_g[l].reshape(1, d_model), ln1_b[l].reshape(1, d_model)
        g2, b2 = ln2_g[l].reshape(1, d_model), ln2_b[l].reshape(1, d_model)

        halves = da_vd // LANES
        dv_slots = tuple(h * halves + half for half in range(halves) for h in range(da_heads))
        plain_slots = tuple(range(SUBLANES))

        def project(x, tag):
            (sq,) = _proj(x, w_in_b, 0, None, "proj_sq_" + tag)
            sk_b, sk = _proj(x, w_in_b, 1, plain_slots, "proj_sk_" + tag)
            sv_b, sv = _proj(x, w_in_b, 2, plain_slots, "proj_sv_" + tag)
            (dq,) = _proj(x, w_in_b, 3, None, "proj_dq_" + tag)
            dk_b, dk = _proj(x, w_in_b, 4, plain_slots, "proj_dk_" + tag)
            dv_b, dv = _proj(x, w_in_b, 5, dv_slots, "proj_dv_" + tag)
            return (sq, sk_b, sv_b, dq, dk_b, dv_b), (sk, sv, dk, dv)

        def dv_tiles_to_rows(tiles, lead):
            return tiles.reshape(lead + (halves, da_heads, LANES)).swapaxes(-3, -2).reshape(lead + (da_heads, da_vd))

        def dv_rows_to_tiles(rows_, n):
            return rows_.reshape(n, da_heads, halves, LANES).swapaxes(1, 2).reshape(n * SUBLANES, LANES)

        (sq, sk, sv, dq, dk, dv), new_p = project(hp, "p")
        sb_o = _sb_prompt(sq, sk, sv, batch, seq, sb_heads, sb_hd)
        da_o = _da_prompt(dq, dk, dv, slopes_arr, lam_vecs, gain, batch, seq, da_heads, da_hd, da_vd, lam_init)
        h1, comb = _post_attn(sb_o, da_o, hp, w_out_b, g1, b1, w_route_b, b_route, alpha, n_groups, per_group)
        hp = _moe(h1, comb, wg_b, wu_b, wd_b, g2, b2, alpha)

        (sq, sk, sv, dq, dk, dv), new_s = project(hs, "s")
        rows = dec_batch * past
        sb_o = _sb_sample(sq, sk, sv, cache_sb_k[l].reshape(rows * SUBLANES, LANES),
                          cache_sb_v[l].reshape(rows * SUBLANES, LANES),
                          dec_batch, dec_seq, past, sb_heads, sb_hd)
        da_o = _da_sample(dq, dk, dv, cache_da_k[l].reshape(rows * SUBLANES, LANES),
                          dv_rows_to_tiles(cache_da_v[l], rows),
                          slopes, lam_vecs, gain, dec_batch, dec_seq, past, da_heads, da_hd, da_vd, lam_init)
        h1, comb = _post_attn(sb_o, da_o, hs, w_out_b, g1, b1, w_route_b, b_route, alpha, n_groups, per_group)
        hs = _moe(h1, comb, wg_b, wu_b, wd_b, g2, b2, alpha)
        for new, lead, rows_out in ((new_p, (batch, seq), rows_p), (new_s, (dec_batch, dec_seq), rows_s)):
            rows_out.append((new[0].reshape(lead + (sb_heads, sb_hd)), new[1].reshape(lead + (sb_heads, sb_hd)),
                             new[2].reshape(lead + (da_heads, 2, da_hd)), dv_tiles_to_rows(new[3], lead)))

    def stack(rows, i):
        return rows[0][i][None] if len(rows) == 1 else jnp.stack([r[i] for r in rows], axis=0)

    return (hp.reshape(batch, seq, d_model), hs.reshape(dec_batch, dec_seq, d_model),
            *(stack(rows_p, i) for i in range(4)), *(stack(rows_s, i) for i in range(4)))
```

```python
import functools
import math

import jax
import jax.numpy as jnp
from jax import lax
from jax.experimental import pallas as pl
from jax.experimental.pallas import tpu as pltpu

F32 = jnp.float32
BF16 = jnp.bfloat16

CHUNK = 64
LN_EPS = 1e-5
RMS_EPS = 1e-5
NEG_INF = -1e30
LOG2E = 1.4426950408889634
LANES = 128
SUBLANES = 8
VMEM_CAP = 64 * 2**20
SB_SKIP_LOG2 = 150.0


def _vmem_limit(block_bytes, scratch_bytes=0):
    est = 2 * block_bytes + scratch_bytes + 16 * 2**20
    return int(min(est, VMEM_CAP - 6 * 2**20))


def _dot(a, b):
    return jnp.dot(a, b, preferred_element_type=F32)


def _dot_nt(a, b):
    return lax.dot_general(a, b, (((1,), (1,)), ((), ())), preferred_element_type=F32)


def _widen(x, width):
    if width < LANES:
        return x[:, :width]
    return x if width == LANES else jnp.concatenate([x] * (width // LANES), axis=1)


def _split_bf16(x):
    hi = x.astype(BF16)
    return hi, (x - hi.astype(F32)).astype(BF16)


def _layer_norm(y, g, b):
    mu = jnp.mean(y, axis=1, keepdims=True)
    yc = y - mu
    var = jnp.mean(yc * yc, axis=1, keepdims=True)
    return yc * lax.rsqrt(var + LN_EPS) * g + b


def _tile_rows(ref, slot, n):
    return ref[pl.ds(slot, n, stride=SUBLANES), :]


def _proj_kernel(x_ref, w_ref, o16_ref, *o32_refs, slots):
    acc = _dot(x_ref[...].astype(BF16), w_ref[...])
    o16_ref[...] = acc.astype(BF16)
    for o32_ref in o32_refs:
        for j, cb in enumerate(slots):
            o32_ref[pl.ds(j, acc.shape[0], stride=SUBLANES), :] = acc[:, cb * LANES:(cb + 1) * LANES]


def _proj(x, w, col_block, slots, name):
    t, d = x.shape
    width = SUBLANES * LANES
    tm = min(512, t)
    assert t % tm == 0
    blk = tm * d * 4 + d * width * 2 + tm * width * (2 + (4 if slots else 0))
    out_specs = [pl.BlockSpec((tm, width), lambda i: (i, 0))]
    out_shape = [jax.ShapeDtypeStruct((t, width), BF16)]
    if slots:
        out_specs.append(pl.BlockSpec((tm * SUBLANES, LANES), lambda i: (i, 0)))
        out_shape.append(jax.ShapeDtypeStruct((t * SUBLANES, LANES), F32))
    return pl.pallas_call(
        functools.partial(_proj_kernel, slots=slots),
        grid=(t // tm,),
        in_specs=[pl.BlockSpec((tm, d), lambda i: (i, 0)),
                  pl.BlockSpec((d, width), lambda i: (0, col_block))],
        out_specs=out_specs,
        out_shape=out_shape,
        compiler_params=pltpu.CompilerParams(
            dimension_semantics=("arbitrary",), vmem_limit_bytes=_vmem_limit(blk)),
        name=name,
    )(x, w)


def _inclusive_suffix_matrix(n):
    r = lax.broadcasted_iota(jnp.int32, (2 * n, n), 0)
    c = lax.broadcasted_iota(jnp.int32, (2 * n, n), 1)
    return (jnp.where(r >= n, r - n, r) >= c).astype(BF16)


def _strict_causal_mask(nq, nk):
    r = lax.broadcasted_iota(jnp.int32, (nq, nk), 0)
    c = lax.broadcasted_iota(jnp.int32, (nq, nk), 1)
    return c < r


def _sb_blocks(qs, ks, vs, carries, accs, umat, mask, scale2):
    n = len(qs)
    nq = qs[0].shape[0]
    zs = [_dot_nt(qs[i], ks[i]) * scale2 for i in range(n)]
    splits = []
    for z in zs:
        neg_abs = lax.bitcast_convert_type(
            lax.bitcast_convert_type(z, jnp.uint32) | jnp.uint32(0x80000000), F32)
        sp = jnp.maximum(z, 0.0) + jnp.log(1.0 + jnp.exp2(neg_abs)) * LOG2E
        if mask is not None:
            sp = jnp.where(mask, sp, 0.0)
        splits.append(jnp.concatenate(_split_bf16(sp), axis=1))
    incl_all = _dot(jnp.concatenate(splits, axis=0) if n > 1 else splits[0], umat)
    out_c, out_a = [], []
    for i in range(n):
        incl = incl_all[i * nq:(i + 1) * nq]
        w = jnp.exp2(zs[i] - incl - carries[i])
        if mask is not None:
            w = jnp.where(mask, w, 0.0)
        out_a.append(accs[i] + _dot(w.astype(BF16), vs[i]))
        out_c.append(carries[i] + incl[:, :1])
    return out_c, out_a


def _all_exhausted(carries):
    lowest = functools.reduce(jnp.minimum, carries)
    return jnp.min(lowest) >= SB_SKIP_LOG2


def _sb_prompt_kernel(q_ref, k_ref, v_ref, o_ref, carry_ref, acc_ref, *, blk, hd, scale2):
    qi = pl.program_id(2)
    nh = q_ref.shape[1] // hd
    umat = _inclusive_suffix_matrix(blk)
    sls = [slice(h * hd, (h + 1) * hd) for h in range(nh)]

    def run(start, carries, accs, mask):
        cs, as_ = _sb_blocks([q_ref[:, sl] for sl in sls], [k_ref[pl.ds(start, blk), sl] for sl in sls],
                             [v_ref[pl.ds(start, blk), sl] for sl in sls], carries, accs, umat, mask, scale2)
        for h in range(nh):
            carry_ref[h] = cs[h]
            acc_ref[:, sls[h]] = as_[h]
        return cs

    run(pl.multiple_of(qi * blk, blk), [jnp.zeros((blk, 1), F32)] * nh, [jnp.zeros((blk, hd), F32)] * nh,
        _strict_causal_mask(blk, blk))

    def cond(state):
        j, go = state
        return jnp.logical_and(j >= 0, go > 0)

    def body(state):
        j, _ = state
        cs = run(pl.multiple_of(j * blk, blk), [carry_ref[h] for h in range(nh)],
                 [acc_ref[:, sl] for sl in sls], None)
        return j - 1, jnp.logical_not(_all_exhausted(cs)).astype(jnp.int32)

    lax.while_loop(cond, body, (qi - 1, jnp.int32(1)))
    o_ref[...] = acc_ref[...].astype(o_ref.dtype)


def _sb_prompt(q, k, v, batch, seq, heads, hd):
    blk = min(256, seq)
    hb = math.gcd(heads, 4)
    assert seq % blk == 0
    nq = seq // blk
    w = hb * hd
    blocks = blk * w * 2 * 2 + 2 * seq * w * 2
    scratch = hb * blk * LANES * 4 + blk * w * 4
    return pl.pallas_call(
        functools.partial(_sb_prompt_kernel, blk=blk, hd=hd, scale2=hd ** -0.5 * LOG2E),
        grid=(batch, heads // hb, nq),
        in_specs=[pl.BlockSpec((blk, w), lambda b, h, i: (b * nq + i, h)),
                  pl.BlockSpec((seq, w), lambda b, h, i: (b, h)),
                  pl.BlockSpec((seq, w), lambda b, h, i: (b, h))],
        out_specs=pl.BlockSpec((blk, w), lambda b, h, i: (b * nq + i, h)),
        out_shape=jax.ShapeDtypeStruct(q.shape, BF16),
        scratch_shapes=[pltpu.VMEM((hb, blk, 1), F32), pltpu.VMEM((blk, w), F32)],
        compiler_params=pltpu.CompilerParams(
            dimension_semantics=("arbitrary", "arbitrary", "arbitrary"),
            vmem_limit_bytes=_vmem_limit(blocks, scratch)),
        name="sb_prompt",
    )(q, k, v)


def _sb_sample_kernel(q_ref, kn_ref, vn_ref, kc_ref, vc_ref, o_ref, carry_ref, acc_ref, u_ref, go_ref,
                      *, heads, hd, scale2):
    kb = pl.program_id(1)
    nq = q_ref.shape[0]
    bk = u_ref.shape[1]
    sls = [slice(h * hd, (h + 1) * hd) for h in range(heads)]

    def run(ks, vs, carries, accs, umat, mask):
        cs, as_ = _sb_blocks([q_ref[:, sl] for sl in sls], ks, vs, carries, accs, umat, mask, scale2)
        for h in range(heads):
            carry_ref[h] = cs[h]
            acc_ref[:, sls[h]] = as_[h]
        go_ref[0] = jnp.logical_not(_all_exhausted(cs)).astype(jnp.int32)

    @pl.when(kb == 0)
    def _():
        u_ref[...] = _inclusive_suffix_matrix(bk)
        run([kn_ref[:, sl] for sl in sls], [vn_ref[:, sl] for sl in sls],
            [jnp.zeros((nq, 1), F32)] * heads, [jnp.zeros((nq, hd), F32)] * heads,
            _inclusive_suffix_matrix(nq), _strict_causal_mask(nq, nq))

    @pl.when(jnp.logical_and(kb > 0, go_ref[0] > 0))
    def _():
        run([_tile_rows(kc_ref, h, bk).astype(BF16) for h in range(heads)],
            [_tile_rows(vc_ref, h, bk).astype(BF16) for h in range(heads)],
            [carry_ref[h] for h in range(heads)], [acc_ref[:, sl] for sl in sls], u_ref[...], None)

    @pl.when(kb == pl.num_programs(1) - 1)
    def _():
        o_ref[...] = acc_ref[...].astype(o_ref.dtype)


def _sb_sample(q, kn, vn, kc, vc, batch, nq, past, heads, hd):
    assert heads == SUBLANES and hd == LANES
    width = heads * hd
    bk = min(512, past)
    assert past % bk == 0
    nkb = past // bk
    new_spec = pl.BlockSpec((nq, width), lambda b, j: (b, 0))
    cache_spec = pl.BlockSpec((bk * SUBLANES, LANES), lambda b, j: (b * nkb + nkb - jnp.maximum(j, 1), 0))
    blocks = 4 * nq * width * 2 + 2 * bk * width * 4
    scratch = heads * nq * LANES * 4 + nq * width * 4 + 2 * bk * bk * 2
    return pl.pallas_call(
        functools.partial(_sb_sample_kernel, heads=heads, hd=hd, scale2=hd ** -0.5 * LOG2E),
        grid=(batch, nkb + 1),
        in_specs=[new_spec, new_spec, new_spec, cache_spec, cache_spec],
        out_specs=new_spec,
        out_shape=jax.ShapeDtypeStruct(q.shape, BF16),
        scratch_shapes=[pltpu.VMEM((heads, nq, 1), F32), pltpu.VMEM((nq, width), F32),
                        pltpu.VMEM((2 * bk, bk), BF16), pltpu.SMEM((1,), jnp.int32)],
        compiler_params=pltpu.CompilerParams(
            dimension_semantics=("arbitrary", "arbitrary"),
            vmem_limit_bytes=_vmem_limit(blocks, scratch)),
        name="sb_sample",
    )(q, kn, vn, kc, vc)


def _da_steps(ss, vs, m_refs, l_refs, acc_refs):
    n = len(ss)
    nk = ss[0].shape[1]
    m_olds = [r[...] for r in m_refs]
    m_news = [jnp.maximum(m_olds[i], jnp.max(ss[i], axis=1, keepdims=True)) for i in range(n)]
    ps = [jnp.exp2(ss[i] - _widen(m_news[i], nk)) for i in range(n)]
    for i in range(n):
        alpha = jnp.exp2(m_olds[i] - m_news[i])
        l_refs[i][...] = alpha * l_refs[i][...] + jnp.sum(ps[i], axis=1, keepdims=True)
        acc_refs[i][...] = (_widen(alpha, vs[i].shape[1]) * acc_refs[i][...]
                            + _dot(ps[i].astype(BF16), vs[i]))
        m_refs[i][...] = m_news[i]


def _da_same_block_bias(nq, nk, slope2):
    r = lax.broadcasted_iota(jnp.int32, (nq, nk), 0)
    c = lax.broadcasted_iota(jnp.int32, (nq, nk), 1)
    return slope2 * (r - jnp.abs(r - c)).astype(F32)


def _da_chunk_mask(nq, nk):
    r = lax.broadcasted_iota(jnp.int32, (nq, nk), 0)
    c = lax.broadcasted_iota(jnp.int32, (nq, nk), 1)
    return (c // CHUNK) <= (r // CHUNK)


def _da_lambda(lam_ref, lam_init):
    lv = lam_ref[...]
    a = jnp.sum(lv[0:1] * lv[1:2], axis=1, keepdims=True)
    b = jnp.sum(lv[2:3] * lv[3:4], axis=1, keepdims=True)
    return jnp.exp(a) - jnp.exp(b) + lam_init


def _da_finish(acc0, l0, acc1, l1, lam, gain, lam_init):
    vd = acc0.shape[1]
    o = acc0 * _widen(1.0 / l0, vd) - lam * (acc1 * _widen(1.0 / l1, vd))
    y = o * lax.rsqrt(jnp.mean(o * o, axis=1, keepdims=True) + RMS_EPS)
    return (y * gain) * (1.0 - lam_init)


def _da_prompt_kernel(slope_ref, q_ref, k_ref, v_ref, lam_ref, g_ref, o_ref, m_ref, l_ref, acc_ref,
                      *, blk, hd, vd, hb, scale2, lam_init):
    hg = pl.program_id(1)
    qi = pl.program_id(2)
    m_ref[...] = jnp.full(m_ref.shape, NEG_INF, F32)
    l_ref[...] = jnp.zeros(l_ref.shape, F32)
    acc_ref[...] = jnp.zeros(acc_ref.shape, F32)
    col = lax.broadcasted_iota(jnp.int32, (1, blk), 1)
    chains = lambda ref: [ref.at[i] for i in range(2 * hb)]
    slopes2 = [slope_ref[hg * hb + h] * LOG2E for h in range(hb)]

    def step(j, bias_of, mask):
        st = pl.multiple_of(j * blk, blk)
        ss, vs = [], []
        for h in range(hb):
            bias = bias_of(slopes2[h])
            v = v_ref[pl.ds(st, blk), h * vd:(h + 1) * vd]
            for m in range(2):
                sl = slice((2 * h + m) * hd, (2 * h + m + 1) * hd)
                s = _dot_nt(q_ref[:, sl], k_ref[pl.ds(st, blk), sl]) * scale2 + bias
                ss.append(s if mask is None else jnp.where(mask, s, NEG_INF))
                vs.append(v)
        _da_steps(ss, vs, chains(m_ref), chains(l_ref), chains(acc_ref))

    def body(j, c):
        rel = (col + (j - qi) * blk).astype(F32)
        step(j, lambda slope2: slope2 * rel, None)
        return c

    lax.fori_loop(0, qi, body, 0)
    step(qi, lambda slope2: _da_same_block_bias(blk, blk, slope2), _da_chunk_mask(blk, blk))
    lam = _da_lambda(lam_ref, lam_init)
    for h in range(hb):
        o_ref[:, h * vd:(h + 1) * vd] = _da_finish(
            acc_ref[2 * h], l_ref[2 * h], acc_ref[2 * h + 1], l_ref[2 * h + 1],
            lam, g_ref[...], lam_init).astype(o_ref.dtype)


def _da_prompt(q, k, v, slopes, lam_vecs, gain, batch, seq, heads, hd, vd, lam_init):
    blk = min(512, seq)
    hb = math.gcd(heads, 2)
    assert seq % blk == 0 and blk % CHUNK == 0
    nq = seq // blk
    blocks = hb * (blk * 2 * hd * 2 + blk * vd * 2 + seq * 2 * hd * 2 + seq * vd * 2)
    scratch = hb * (2 * 2 * blk * LANES * 4 + 2 * blk * vd * 4)
    return pl.pallas_call(
        functools.partial(_da_prompt_kernel, blk=blk, hd=hd, vd=vd, hb=hb, scale2=hd ** -0.5 * LOG2E,
                          lam_init=lam_init),
        grid=(batch, heads // hb, nq),
        in_specs=[pl.BlockSpec(memory_space=pltpu.SMEM),
                  pl.BlockSpec((blk, hb * 2 * hd), lambda b, h, i: (b * nq + i, h)),
                  pl.BlockSpec((seq, hb * 2 * hd), lambda b, h, i: (b, h)),
                  pl.BlockSpec((seq, hb * vd), lambda b, h, i: (b, h)),
                  pl.BlockSpec((4, hd), lambda b, h, i: (0, 0)),
                  pl.BlockSpec((1, vd), lambda b, h, i: (0, 0))],
        out_specs=pl.BlockSpec((blk, hb * vd), lambda b, h, i: (b * nq + i, h)),
        out_shape=jax.ShapeDtypeStruct(v.shape, BF16),
        scratch_shapes=[pltpu.VMEM((2 * hb, blk, LANES), F32), pltpu.VMEM((2 * hb, blk, LANES), F32),
                        pltpu.VMEM((2 * hb, blk, vd), F32)],
        compiler_params=pltpu.CompilerParams(
            dimension_semantics=("arbitrary", "arbitrary", "arbitrary"),
            vmem_limit_bytes=_vmem_limit(blocks, scratch)),
        name="da_prompt",
    )(slopes, q, k, v, lam_vecs, gain)


def _da_sample_kernel(q_ref, kn_ref, vn_ref, kc_ref, vc_ref, lam_ref, g_ref, o_ref,
                      m_ref, l_ref, acc_ref, *, heads, hd, vd, past, slopes, scale2, lam_init):
    kb = pl.program_id(1)
    nkb = pl.num_programs(1) - 1
    nq = q_ref.shape[0]
    bk = kc_ref.shape[0] // SUBLANES
    chains = 2 * heads
    refs = lambda ref: [ref.at[i] for i in range(chains)]

    @pl.when(kb == 0)
    def _():
        m_ref[...] = jnp.full(m_ref.shape, NEG_INF, F32)
        l_ref[...] = jnp.zeros(l_ref.shape, F32)
        acc_ref[...] = jnp.zeros(acc_ref.shape, F32)

    def step(k_of, v_of, bias_of):
        ss, vs = [], []
        for h in range(heads):
            v = v_of(h)
            bias = bias_of(slopes[h] * LOG2E)
            for m in range(2):
                i = 2 * h + m
                ss.append(_dot_nt(q_ref[:, i * hd:(i + 1) * hd], k_of(i)) * scale2 + bias)
                vs.append(v)
        _da_steps(ss, vs, refs(m_ref), refs(l_ref), refs(acc_ref))

    @pl.when(kb < nkb)
    def _():
        col = lax.broadcasted_iota(jnp.int32, (1, bk), 1)
        rel = (col + (kb * bk - past)).astype(F32)
        step(lambda i: _tile_rows(kc_ref, i, bk).astype(BF16),
             lambda h: jnp.concatenate([_tile_rows(vc_ref, half * heads + h, bk) for half in range(vd // LANES)],
                                       axis=1).astype(BF16),
             lambda slope2: slope2 * rel)

    @pl.when(kb == nkb)
    def _():
        step(lambda i: kn_ref[:, i * hd:(i + 1) * hd], lambda h: vn_ref[:, h * vd:(h + 1) * vd],
             lambda slope2: _da_same_block_bias(nq, nq, slope2))
        lam = _da_lambda(lam_ref, lam_init)
        for h in range(heads):
            o_ref[:, h * vd:(h + 1) * vd] = _da_finish(
                acc_ref[2 * h], l_ref[2 * h], acc_ref[2 * h + 1], l_ref[2 * h + 1],
                lam, g_ref[...], lam_init).astype(o_ref.dtype)


def _da_sample(q, kn, vn, kc, vc, slopes, lam_vecs, gain, batch, nq, past, heads, hd, vd, lam_init):
    assert nq == CHUNK and past % CHUNK == 0
    assert hd == LANES and 2 * heads == SUBLANES and heads * (vd // LANES) == SUBLANES
    qk_w = heads * 2 * hd
    v_w = heads * vd
    bk = min(512, past)
    assert past % bk == 0
    nkb = past // bk
    blocks = 2 * nq * qk_w * 2 + 2 * nq * v_w * 2 + bk * (qk_w + v_w) * 4
    scratch = 2 * 2 * heads * nq * LANES * 4 + 2 * heads * nq * vd * 4
    cache_idx = lambda b, j: (b * nkb + jnp.minimum(j, nkb - 1), 0)
    return pl.pallas_call(
        functools.partial(_da_sample_kernel, heads=heads, hd=hd, vd=vd, past=past, slopes=slopes,
                          scale2=hd ** -0.5 * LOG2E, lam_init=lam_init),
        grid=(batch, nkb + 1),
        in_specs=[pl.BlockSpec((nq, qk_w), lambda b, j: (b, 0)),
                  pl.BlockSpec((nq, qk_w), lambda b, j: (b, 0)),
                  pl.BlockSpec((nq, v_w), lambda b, j: (b, 0)),
                  pl.BlockSpec((bk * SUBLANES, LANES), cache_idx),
                  pl.BlockSpec((bk * SUBLANES, LANES), cache_idx),
                  pl.BlockSpec((4, hd), lambda b, j: (0, 0)),
                  pl.BlockSpec((1, vd), lambda b, j: (0, 0))],
        out_specs=pl.BlockSpec((nq, v_w), lambda b, j: (b, 0)),
        out_shape=jax.ShapeDtypeStruct(vn.shape, BF16),
        scratch_shapes=[pltpu.VMEM((2 * heads, nq, LANES), F32), pltpu.VMEM((2 * heads, nq, LANES), F32),
                        pltpu.VMEM((2 * heads, nq, vd), F32)],
        compiler_params=pltpu.CompilerParams(
            dimension_semantics=("arbitrary", "arbitrary"),
            vmem_limit_bytes=_vmem_limit(blocks, scratch)),
        name="da_sample",
    )(q, kn, vn, kc, vc, lam_vecs, gain)


def _route(logits, n_groups, per_group):
    n_exp = n_groups * per_group
    lane = lax.broadcasted_iota(jnp.int32, logits.shape, 1).astype(F32)
    big = jnp.float32(3e38)

    def first_argmax(x):
        top = jnp.max(x, axis=1, keepdims=True)
        idx = jnp.min(jnp.where(x == top, lane, big), axis=1, keepdims=True)
        return top, idx

    is_coarse = (lane >= n_exp) & (lane < n_exp + n_groups)
    coarse = jnp.where(is_coarse, logits, -big)
    c_top, c_idx = first_argmax(coarse)
    g_gate = 1.0 / jnp.sum(jnp.where(is_coarse, jnp.exp(logits - c_top), 0.0), axis=1, keepdims=True)
    g_first = (c_idx - n_exp) * per_group
    in_group = (lane >= g_first) & (lane < g_first + per_group)
    fine = jnp.where(in_group, logits, -big)
    v1, i1 = first_argmax(fine)
    v2, i2 = first_argmax(jnp.where(lane == i1, -big, fine))
    e2 = jnp.exp(v2 - v1)
    w1 = g_gate / (1.0 + e2)
    w2 = g_gate * e2 / (1.0 + e2)
    comb = jnp.where(lane == i1, w1, jnp.where(lane == i2, w2, 0.0))
    return jnp.where(lane == n_exp, c_idx - n_exp, comb)


def _post_attn_kernel(sb_ref, da_ref, x_ref, wsb_ref, wda_ref, g_ref, b_ref, wr_ref, br_ref,
                      h1_ref, comb_ref, *, alpha, n_groups, per_group):
    mixed = _dot(sb_ref[...], wsb_ref[...]) + _dot(da_ref[...], wda_ref[...])
    h1 = _layer_norm(alpha * x_ref[...] + mixed, g_ref[...], b_ref[...])
    h1_ref[...] = h1
    h_hi, h_lo = _split_bf16(h1)
    both = _dot(h_hi, wr_ref[...])
    logits = both[:, :LANES] + both[:, LANES:] + _dot(h_lo, wr_ref[:, :LANES]) + br_ref[...]
    comb_ref[...] = _route(logits, n_groups, per_group)


def _post_attn(sb_o, da_o, x, w_out, ln_g, ln_b, w_route, b_route, alpha, n_groups, per_group):
    t, d = x.shape
    w_sb = sb_o.shape[1]
    w_da = da_o.shape[1]
    assert w_sb == w_da
    tm = min(256, t)
    assert t % tm == 0
    blocks = (tm * (w_sb + w_da) * 2 + tm * d * 4 * 2 + (w_sb + w_da) * d * 2
              + d * 2 * LANES * 2 + tm * LANES * 4)
    row = lambda i: (i, 0)
    fixed = lambda i: (0, 0)
    return pl.pallas_call(
        functools.partial(_post_attn_kernel, alpha=alpha, n_groups=n_groups, per_group=per_group),
        grid=(t // tm,),
        in_specs=[pl.BlockSpec((tm, w_sb), row), pl.BlockSpec((tm, w_da), row), pl.BlockSpec((tm, d), row),
                  pl.BlockSpec((w_sb, d), fixed), pl.BlockSpec((w_da, d), lambda i: (1, 0)),
                  pl.BlockSpec((1, d), fixed), pl.BlockSpec((1, d), fixed),
                  pl.BlockSpec((d, 2 * LANES), fixed), pl.BlockSpec((1, LANES), fixed)],
        out_specs=[pl.BlockSpec((tm, d), row), pl.BlockSpec((tm, LANES), row)],
        out_shape=[jax.ShapeDtypeStruct((t, d), F32), jax.ShapeDtypeStruct((t, LANES), F32)],
        compiler_params=pltpu.CompilerParams(
            dimension_semantics=("arbitrary",), vmem_limit_bytes=_vmem_limit(blocks)),
        name="post_attn",
    )(sb_o, da_o, x, w_out, w_out, ln_g, ln_b, w_route, b_route)


def _dispatch_plan(comb, n_exp, n_groups, tm):
    t = comb.shape[0]
    n_tiles = t // tm + n_groups
    group = comb[:, n_exp].astype(jnp.int32)
    counts = jnp.sum((group[:, None] == jnp.arange(n_groups)[None, :]).astype(jnp.int32), axis=0)
    tiles_g = (counts + tm - 1) // tm
    tile_end = jnp.cumsum(tiles_g)
    tile_start = tile_end - tiles_g
    count_start = jnp.cumsum(counts) - counts
    tile_ids = jnp.arange(n_tiles, dtype=jnp.int32)
    tile_group = jnp.minimum(jnp.sum((tile_ids[:, None] >= tile_end[None, :]).astype(jnp.int32), axis=1),
                             n_groups - 1)
    first_row = (tile_ids - tile_start[tile_group]) * tm
    tile_valid = jnp.clip(counts[tile_group] - first_row, 0, tm)
    order = jnp.argsort(group, stable=True).astype(jnp.int32)
    slot_group = jnp.repeat(tile_group, tm)
    rank = jnp.repeat(first_row, tm) + jnp.tile(jnp.arange(tm, dtype=jnp.int32), n_tiles)
    rank = jnp.minimum(rank, counts[slot_group] - 1)
    src = order[jnp.clip(count_start[slot_group] + rank, 0, t - 1)]
    return src, tile_group.astype(jnp.int32), tile_valid.astype(jnp.int32), tile_end[-1:].astype(jnp.int32)


def _moe_kernel(src_ref, grp_ref, valid_ref, used_ref, h1_hbm, comb_ref, wg_ref, wu_ref, wd_ref, g_ref, b_ref,
                y_hbm, xbuf, xb_ref, acc_ref, obuf, sem_in, sem_out, *, alpha, tm, per_group):
    i = pl.program_id(0)
    j = pl.program_id(1)
    n_used = used_ref[0]
    slot = lax.rem(i, 2)
    live = i < n_used

    def row_in(tile, r, buf_slot):
        return pltpu.make_async_copy(h1_hbm.at[pl.ds(src_ref[tile * tm + r], 1), :],
                                     xbuf.at[buf_slot, pl.ds(r, 1), :], sem_in.at[buf_slot])

    def row_out(tile, r):
        return pltpu.make_async_copy(obuf.at[pl.ds(r, 1), :],
                                     y_hbm.at[pl.ds(src_ref[tile * tm + r], 1), :], sem_out)

    def gather(tile, buf_slot):
        def issue(r, c):
            row_in(tile, r, buf_slot).start()
            return c
        lax.fori_loop(0, tm, issue, 0, unroll=8)

    def drain_scatter(tile):
        def wait(r, c):
            row_out(tile, 0).wait()
            return c
        lax.fori_loop(0, valid_ref[tile], wait, 0)

    @pl.when(jnp.logical_and(live, j == 0))
    def _():
        @pl.when(i == 0)
        def _():
            gather(0, 0)
        pltpu.make_async_copy(h1_hbm.at[pl.ds(0, tm), :], xbuf.at[slot], sem_in.at[slot]).wait()

        @pl.when(i + 1 < n_used)
        def _():
            gather(i + 1, 1 - slot)
        xb_ref[...] = xbuf[slot].astype(BF16)
        acc_ref[...] = jnp.zeros(acc_ref.shape, F32)

    @pl.when(live)
    def _():
        xb = xb_ref[...]
        gate = _dot(xb, wg_ref[...])
        up = _dot(xb, wu_ref[...])
        hidden = (gate / (1.0 + jnp.exp(-gate))) * up
        lane = lax.broadcasted_iota(jnp.int32, comb_ref.shape, 1)
        expert = grp_ref[i] * per_group + j
        weight = jnp.sum(jnp.where(lane == expert, comb_ref[...], 0.0), axis=1, keepdims=True)
        acc_ref[...] += weight * _dot(hidden.astype(BF16), wd_ref[...])

    @pl.when(jnp.logical_and(live, j == per_group - 1))
    def _():
        @pl.when(i > 0)
        def _():
            drain_scatter(i - 1)
        obuf[...] = _layer_norm(alpha * xbuf[slot] + acc_ref[...], g_ref[...], b_ref[...])

        def issue(r, c):
            row_out(i, r).start()
            return c
        lax.fori_loop(0, valid_ref[i], issue, 0)

        @pl.when(i == n_used - 1)
        def _():
            drain_scatter(i)


def _moe(h1, comb, w_gate, w_up, w_down, ln_g, ln_b, alpha, n_groups):
    t, d = h1.shape
    n_exp, _, d_e = w_gate.shape
    per_group = n_exp // n_groups
    tm = min(512, t)
    assert t % tm == 0
    src, tile_group, tile_valid, n_used = _dispatch_plan(comb, n_exp, n_groups, tm)
    n_tiles = tile_group.shape[0]
    comb_sorted = comb[src]
    blocks = tm * LANES * 4 + 3 * d * d_e * 2
    scratch = 2 * tm * d * 4 + tm * d * 2 + 2 * tm * d * 4
    expert_idx = lambda i, j, src, grp, valid, used: (grp[i] * per_group + j, 0, 0)
    fixed = lambda i, j, src, grp, valid, used: (0, 0)
    return pl.pallas_call(
        functools.partial(_moe_kernel, alpha=alpha, tm=tm, per_group=per_group),
        grid_spec=pltpu.PrefetchScalarGridSpec(
            num_scalar_prefetch=4,
            grid=(n_tiles, per_group),
            in_specs=[pl.BlockSpec(memory_space=pl.ANY),
                      pl.BlockSpec((tm, LANES), lambda i, j, src, grp, valid, used: (i, 0)),
                      pl.BlockSpec((None, d, d_e), expert_idx),
                      pl.BlockSpec((None, d, d_e), expert_idx),
                      pl.BlockSpec((None, d_e, d), expert_idx),
                      pl.BlockSpec((1, d), fixed), pl.BlockSpec((1, d), fixed)],
            out_specs=pl.BlockSpec(memory_space=pl.ANY),
            scratch_shapes=[pltpu.VMEM((2, tm, d), F32), pltpu.VMEM((tm, d), BF16), pltpu.VMEM((tm, d), F32),
                            pltpu.VMEM((tm, d), F32), pltpu.SemaphoreType.DMA((2,)), pltpu.SemaphoreType.DMA]),
        out_shape=jax.ShapeDtypeStruct((t, d), F32),
        compiler_params=pltpu.CompilerParams(
            dimension_semantics=("arbitrary", "arbitrary"),
            vmem_limit_bytes=_vmem_limit(blocks, scratch)),
        name="moe",
    )(src, tile_group, tile_valid, n_used, h1, comb_sorted, w_gate, w_up, w_down, ln_g, ln_b)


def kernel(x_prompt, x_sample, cache_sb_k, cache_sb_v, cache_da_k, cache_da_v, w_in, w_out, lambda_q1, lambda_k1, lambda_q2, lambda_k2, subln_g, ln1_g, ln1_b, w_coarse, b_coarse, w_fine, b_fine, w_gate, w_up, w_down, ln2_g, ln2_b):
    depth, dec_batch, past, sb_heads, sb_hd = cache_sb_k.shape
    _, _, _, da_heads, _, da_hd = cache_da_k.shape
    da_vd = cache_da_v.shape[-1]
    batch, seq, d_model = x_prompt.shape
    dec_seq = x_sample.shape[1]
    n_groups, per_group = w_fine.shape[2], w_fine.shape[3]
    n_exp = n_groups * per_group
    assert n_exp + n_groups <= LANES
    sb_w = sb_heads * sb_hd
    assert sb_w == da_heads * 2 * da_hd == da_heads * da_vd == SUBLANES * LANES, \
        "projection column groups are addressed as equal-width blocks of one token tile"
    alpha = (2 * depth) ** 0.25
    slopes = tuple(2.0 ** (-8.0 * (h + 1) / da_heads) for h in range(da_heads))
    slopes_arr = jnp.asarray(slopes, F32)

    hp = x_prompt.reshape(batch * seq, d_model)
    hs = x_sample.reshape(dec_batch * dec_seq, d_model)
    rows_p, rows_s = [], []
    for l in range(depth):
        lam_init = 0.8 - 0.6 * math.exp(-0.3 * l)
        w_in_b = w_in[l].astype(BF16)
        w_out_b = w_out[l].astype(BF16)
        lam_vecs = jnp.stack([lambda_q1[l], lambda_k1[l], lambda_q2[l], lambda_k2[l]]).astype(F32)
        gain = subln_g[l].reshape(1, da_vd)
        w_route = jnp.concatenate(
            [w_fine[l].reshape(d_model, n_exp), w_coarse[l],
             jnp.zeros((d_model, LANES - n_exp - n_groups), F32)], axis=1)
        w_route_b = jnp.concatenate(_split_bf16(w_route), axis=1)
        b_route = jnp.concatenate(
            [b_fine[l].reshape(n_exp), b_coarse[l], jnp.zeros((LANES - n_exp - n_groups,), F32)]).reshape(1, LANES)
        wg_b, wu_b, wd_b = w_gate[l].astype(BF16), w_up[l].astype(BF16), w_down[l].astype(BF16)
        g1, b1 = ln1_g[l].reshape(1, d_model), ln1_b[l].reshape(1, d_model)
        g2, b2 = ln2_g[l].reshape(1, d_model), ln2_b[l].reshape(1, d_model)

        halves = da_vd // LANES
        dv_slots = tuple(h * halves + half for half in range(halves) for h in range(da_heads))
        plain_slots = tuple(range(SUBLANES))

        def project(x, tag):
            (sq,) = _proj(x, w_in_b, 0, None, "proj_sq_" + tag)
            sk_b, sk = _proj(x, w_in_b, 1, plain_slots, "proj_sk_" + tag)
            sv_b, sv = _proj(x, w_in_b, 2, plain_slots, "proj_sv_" + tag)
            (dq,) = _proj(x, w_in_b, 3, None, "proj_dq_" + tag)
            dk_b, dk = _proj(x, w_in_b, 4, plain_slots, "proj_dk_" + tag)
            dv_b, dv = _proj(x, w_in_b, 5, dv_slots, "proj_dv_" + tag)
            return (sq, sk_b, sv_b, dq, dk_b, dv_b), (sk, sv, dk, dv)

        def dv_tiles_to_rows(tiles, lead):
            return tiles.reshape(lead + (halves, da_heads, LANES)).swapaxes(-3, -2).reshape(lead + (da_heads, da_vd))

        def dv_rows_to_tiles(rows_, n):
            return rows_.reshape(n, da_heads, halves, LANES).swapaxes(1, 2).reshape(n * SUBLANES, LANES)

        (sq, sk, sv, dq, dk, dv), new_p = project(hp, "p")
        sb_o = _sb_prompt(sq, sk, sv, batch, seq, sb_heads, sb_hd)
        da_o = _da_prompt(dq, dk, dv, slopes_arr, lam_vecs, gain, batch, seq, da_heads, da_hd, da_vd, lam_init)
        h1, comb = _post_attn(sb_o, da_o, hp, w_out_b, g1, b1, w_route_b, b_route, alpha, n_groups, per_group)
        hp = _moe(h1, comb, wg_b, wu_b, wd_b, g2, b2, alpha, n_groups)

        (sq, sk, sv, dq, dk, dv), new_s = project(hs, "s")
        rows = dec_batch * past
        sb_o = _sb_sample(sq, sk, sv, cache_sb_k[l].reshape(rows * SUBLANES, LANES),
                          cache_sb_v[l].reshape(rows * SUBLANES, LANES),
                          dec_batch, dec_seq, past, sb_heads, sb_hd)
        da_o = _da_sample(dq, dk, dv, cache_da_k[l].reshape(rows * SUBLANES, LANES),
                          dv_rows_to_tiles(cache_da_v[l], rows),
                          slopes, lam_vecs, gain, dec_batch, dec_seq, past, da_heads, da_hd, da_vd, lam_init)
        h1, comb = _post_attn(sb_o, da_o, hs, w_out_b, g1, b1, w_route_b, b_route, alpha, n_groups, per_group)
        hs = _moe(h1, comb, wg_b, wu_b, wd_b, g2, b2, alpha, n_groups)
        for new, lead, rows_out in ((new_p, (batch, seq), rows_p), (new_s, (dec_batch, dec_seq), rows_s)):
            rows_out.append((new[0].reshape(lead + (sb_heads, sb_hd)), new[1].reshape(lead + (sb_heads, sb_hd)),
                             new[2].reshape(lead + (da_heads, 2, da_hd)), dv_tiles_to_rows(new[3], lead)))

    def stack(rows, i):
        return rows[0][i][None] if len(rows) == 1 else jnp.stack([r[i] for r in rows], axis=0)

    return (hp.reshape(batch, seq, d_model), hs.reshape(dec_batch, dec_seq, d_model),
            *(stack(rows_p, i) for i in range(4)), *(stack(rows_s, i) for i in range(4)))
```

```python
import functools
import math

import jax
import jax.numpy as jnp
from jax import lax
from jax.experimental import pallas as pl
from jax.experimental.pallas import tpu as pltpu

F32 = jnp.float32
BF16 = jnp.bfloat16

CHUNK = 64
LN_EPS = 1e-5
RMS_EPS = 1e-5
NEG_INF = -1e30
LOG2E = 1.4426950408889634
LANES = 128
SUBLANES = 8
VMEM_CAP = 64 * 2**20
SB_SKIP_LOG2 = 150.0


def _vmem_limit(block_bytes, scratch_bytes=0):
    est = 2 * block_bytes + scratch_bytes + 16 * 2**20
    return int(min(est, VMEM_CAP - 6 * 2**20))


def _dot(a, b):
    return jnp.dot(a, b, preferred_element_type=F32)


def _dot_nt(a, b):
    return lax.dot_general(a, b, (((1,), (1,)), ((), ())), preferred_element_type=F32)


def _widen(x, width):
    if width < LANES:
        return x[:, :width]
    return x if width == LANES else jnp.concatenate([x] * (width // LANES), axis=1)


def _split_bf16(x):
    hi = x.astype(BF16)
    return hi, (x - hi.astype(F32)).astype(BF16)


def _layer_norm(y, g, b):
    mu = jnp.mean(y, axis=1, keepdims=True)
    yc = y - mu
    var = jnp.mean(yc * yc, axis=1, keepdims=True)
    return yc * lax.rsqrt(var + LN_EPS) * g + b


def _tile_rows(ref, slot, n):
    return ref[pl.ds(slot, n, stride=SUBLANES), :]


def _proj_kernel(x_ref, w_ref, *out_refs, groups):
    xb = x_ref[...].astype(BF16)
    width = SUBLANES * LANES
    o32_refs = iter(out_refs[len(groups):])
    for g, (scale, slots) in enumerate(groups):
        acc = _dot(xb, w_ref[:, g * width:(g + 1) * width])
        out_refs[g][...] = (acc if scale is None else acc * scale).astype(BF16)
        if slots:
            o32_ref = next(o32_refs)
            for j, cb in enumerate(slots):
                o32_ref[pl.ds(j, acc.shape[0], stride=SUBLANES), :] = acc[:, cb * LANES:(cb + 1) * LANES]


def _proj(x, w, first_group, groups, name):
    t, d = x.shape
    width = SUBLANES * LANES
    n = len(groups)
    assert first_group % n == 0
    tm = min(512, t)
    assert t % tm == 0
    n32 = sum(1 for _, slots in groups if slots)
    blk = tm * d * 4 + d * n * width * 2 + tm * width * (2 * n + 4 * n32)
    out_specs = ([pl.BlockSpec((tm, width), lambda i: (i, 0))] * n
                 + [pl.BlockSpec((tm * SUBLANES, LANES), lambda i: (i, 0))] * n32)
    out_shape = ([jax.ShapeDtypeStruct((t, width), BF16)] * n
                 + [jax.ShapeDtypeStruct((t * SUBLANES, LANES), F32)] * n32)
    return pl.pallas_call(
        functools.partial(_proj_kernel, groups=groups),
        grid=(t // tm,),
        in_specs=[pl.BlockSpec((tm, d), lambda i: (i, 0)),
                  pl.BlockSpec((d, n * width), lambda i: (0, first_group // n))],
        out_specs=out_specs,
        out_shape=out_shape,
        compiler_params=pltpu.CompilerParams(
            dimension_semantics=("arbitrary",), vmem_limit_bytes=_vmem_limit(blk)),
        name=name,
    )(x, w)


def _inclusive_suffix_matrix(n):
    r = lax.broadcasted_iota(jnp.int32, (2 * n, n), 0)
    c = lax.broadcasted_iota(jnp.int32, (2 * n, n), 1)
    return (jnp.where(r >= n, r - n, r) >= c).astype(BF16)


def _strict_causal_mask(nq, nk):
    r = lax.broadcasted_iota(jnp.int32, (nq, nk), 0)
    c = lax.broadcasted_iota(jnp.int32, (nq, nk), 1)
    return c < r


def _sb_blocks(qs, ks, vs, carries, accs, umat, mask):
    n = len(qs)
    nq = qs[0].shape[0]
    zs = [_dot_nt(qs[i], ks[i]) for i in range(n)]
    splits = []
    for z in zs:
        neg_abs = lax.bitcast_convert_type(
            lax.bitcast_convert_type(z, jnp.uint32) | jnp.uint32(0x80000000), F32)
        sp = jnp.maximum(z, 0.0) + jnp.log(1.0 + jnp.exp2(neg_abs)) * LOG2E
        if mask is not None:
            sp = jnp.where(mask, sp, 0.0)
        splits.append(jnp.concatenate(_split_bf16(sp), axis=1))
    incl_all = _dot(jnp.concatenate(splits, axis=0) if n > 1 else splits[0], umat)
    out_c, out_a = [], []
    for i in range(n):
        incl = incl_all[i * nq:(i + 1) * nq]
        w = jnp.exp2(zs[i] - incl - carries[i])
        if mask is not None:
            w = jnp.where(mask, w, 0.0)
        out_a.append(accs[i] + _dot(w.astype(BF16), vs[i]))
        out_c.append(carries[i] + incl[:, :1])
    return out_c, out_a


def _all_exhausted(carries):
    lowest = functools.reduce(jnp.minimum, carries)
    return jnp.min(lowest) >= SB_SKIP_LOG2


def _sb_prompt_kernel(q_ref, k_ref, v_ref, o_ref, carry_ref, acc_ref, *, blk, hd):
    qi = pl.program_id(2)
    nh = q_ref.shape[1] // hd
    umat = _inclusive_suffix_matrix(blk)
    sls = [slice(h * hd, (h + 1) * hd) for h in range(nh)]

    def run(start, carries, accs, mask):
        cs, as_ = _sb_blocks([q_ref[:, sl] for sl in sls], [k_ref[pl.ds(start, blk), sl] for sl in sls],
                             [v_ref[pl.ds(start, blk), sl] for sl in sls], carries, accs, umat, mask)
        for h in range(nh):
            carry_ref[h] = cs[h]
            acc_ref[:, sls[h]] = as_[h]
        return cs

    run(pl.multiple_of(qi * blk, blk), [jnp.zeros((blk, 1), F32)] * nh, [jnp.zeros((blk, hd), F32)] * nh,
        _strict_causal_mask(blk, blk))

    def cond(state):
        j, go = state
        return jnp.logical_and(j >= 0, go > 0)

    def body(state):
        j, _ = state
        cs = run(pl.multiple_of(j * blk, blk), [carry_ref[h] for h in range(nh)],
                 [acc_ref[:, sl] for sl in sls], None)
        return j - 1, jnp.logical_not(_all_exhausted(cs)).astype(jnp.int32)

    lax.while_loop(cond, body, (qi - 1, jnp.int32(1)))
    o_ref[...] = acc_ref[...].astype(o_ref.dtype)


def _sb_prompt(q, k, v, batch, seq, heads, hd):
    blk = min(256, seq)
    hb = math.gcd(heads, 4)
    assert seq % blk == 0
    nq = seq // blk
    w = hb * hd
    blocks = blk * w * 2 * 2 + 2 * seq * w * 2
    scratch = hb * blk * LANES * 4 + blk * w * 4
    return pl.pallas_call(
        functools.partial(_sb_prompt_kernel, blk=blk, hd=hd),
        grid=(batch, heads // hb, nq),
        in_specs=[pl.BlockSpec((blk, w), lambda b, h, i: (b * nq + i, h)),
                  pl.BlockSpec((seq, w), lambda b, h, i: (b, h)),
                  pl.BlockSpec((seq, w), lambda b, h, i: (b, h))],
        out_specs=pl.BlockSpec((blk, w), lambda b, h, i: (b * nq + i, h)),
        out_shape=jax.ShapeDtypeStruct(q.shape, BF16),
        scratch_shapes=[pltpu.VMEM((hb, blk, 1), F32), pltpu.VMEM((blk, w), F32)],
        compiler_params=pltpu.CompilerParams(
            dimension_semantics=("arbitrary", "arbitrary", "arbitrary"),
            vmem_limit_bytes=_vmem_limit(blocks, scratch)),
        name="sb_prompt",
    )(q, k, v)


def _sb_sample_kernel(q_ref, kn_ref, vn_ref, kc_ref, vc_ref, o_ref, carry_ref, acc_ref, u_ref, go_ref,
                      *, heads, hd):
    kb = pl.program_id(1)
    nq = q_ref.shape[0]
    bk = u_ref.shape[1]
    sls = [slice(h * hd, (h + 1) * hd) for h in range(heads)]

    def run(ks, vs, carries, accs, umat, mask):
        cs, as_ = _sb_blocks([q_ref[:, sl] for sl in sls], ks, vs, carries, accs, umat, mask)
        for h in range(heads):
            carry_ref[h] = cs[h]
            acc_ref[:, sls[h]] = as_[h]
        go_ref[0] = jnp.logical_not(_all_exhausted(cs)).astype(jnp.int32)

    @pl.when(kb == 0)
    def _():
        u_ref[...] = _inclusive_suffix_matrix(bk)
        run([kn_ref[:, sl] for sl in sls], [vn_ref[:, sl] for sl in sls],
            [jnp.zeros((nq, 1), F32)] * heads, [jnp.zeros((nq, hd), F32)] * heads,
            _inclusive_suffix_matrix(nq), _strict_causal_mask(nq, nq))

    @pl.when(jnp.logical_and(kb > 0, go_ref[0] > 0))
    def _():
        run([_tile_rows(kc_ref, h, bk).astype(BF16) for h in range(heads)],
            [_tile_rows(vc_ref, h, bk).astype(BF16) for h in range(heads)],
            [carry_ref[h] for h in range(heads)], [acc_ref[:, sl] for sl in sls], u_ref[...], None)

    @pl.when(kb == pl.num_programs(1) - 1)
    def _():
        o_ref[...] = acc_ref[...].astype(o_ref.dtype)


def _sb_sample(q, kn, vn, kc, vc, batch, nq, past, heads, hd):
    assert heads == SUBLANES and hd == LANES
    width = heads * hd
    bk = min(512, past)
    assert past % bk == 0
    nkb = past // bk
    new_spec = pl.BlockSpec((nq, width), lambda b, j: (b, 0))
    cache_spec = pl.BlockSpec((bk * SUBLANES, LANES), lambda b, j: (b * nkb + nkb - jnp.maximum(j, 1), 0))
    blocks = 4 * nq * width * 2 + 2 * bk * width * 4
    scratch = heads * nq * LANES * 4 + nq * width * 4 + 2 * bk * bk * 2
    return pl.pallas_call(
        functools.partial(_sb_sample_kernel, heads=heads, hd=hd),
        grid=(batch, nkb + 1),
        in_specs=[new_spec, new_spec, new_spec, cache_spec, cache_spec],
        out_specs=new_spec,
        out_shape=jax.ShapeDtypeStruct(q.shape, BF16),
        scratch_shapes=[pltpu.VMEM((heads, nq, 1), F32), pltpu.VMEM((nq, width), F32),
                        pltpu.VMEM((2 * bk, bk), BF16), pltpu.SMEM((1,), jnp.int32)],
        compiler_params=pltpu.CompilerParams(
            dimension_semantics=("arbitrary", "arbitrary"),
            vmem_limit_bytes=_vmem_limit(blocks, scratch)),
        name="sb_sample",
    )(q, kn, vn, kc, vc)


def _da_steps(ss, vs, m_refs, l_refs, acc_refs):
    n = len(ss)
    nk = ss[0].shape[1]
    m_olds = [r[...] for r in m_refs]
    m_news = [jnp.maximum(m_olds[i], jnp.max(ss[i], axis=1, keepdims=True)) for i in range(n)]
    ps = [jnp.exp2(ss[i] - _widen(m_news[i], nk)) for i in range(n)]
    for i in range(n):
        alpha = jnp.exp2(m_olds[i] - m_news[i])
        l_refs[i][...] = alpha * l_refs[i][...] + jnp.sum(ps[i], axis=1, keepdims=True)
        acc_refs[i][...] = (_widen(alpha, vs[i].shape[1]) * acc_refs[i][...]
                            + _dot(ps[i].astype(BF16), vs[i]))
        m_refs[i][...] = m_news[i]


def _da_same_block_bias(nq, nk, slope2):
    r = lax.broadcasted_iota(jnp.int32, (nq, nk), 0)
    c = lax.broadcasted_iota(jnp.int32, (nq, nk), 1)
    return slope2 * (r - jnp.abs(r - c)).astype(F32)


def _da_chunk_mask(nq, nk):
    r = lax.broadcasted_iota(jnp.int32, (nq, nk), 0)
    c = lax.broadcasted_iota(jnp.int32, (nq, nk), 1)
    return (c // CHUNK) <= (r // CHUNK)


def _da_lambda(lam_ref, lam_init):
    lv = lam_ref[...]
    a = jnp.sum(lv[0:1] * lv[1:2], axis=1, keepdims=True)
    b = jnp.sum(lv[2:3] * lv[3:4], axis=1, keepdims=True)
    return jnp.exp(a) - jnp.exp(b) + lam_init


def _da_finish(acc0, l0, acc1, l1, lam, gain, lam_init):
    vd = acc0.shape[1]
    o = acc0 * _widen(1.0 / l0, vd) - lam * (acc1 * _widen(1.0 / l1, vd))
    y = o * lax.rsqrt(jnp.mean(o * o, axis=1, keepdims=True) + RMS_EPS)
    return (y * gain) * (1.0 - lam_init)


def _da_prompt_kernel(slope_ref, q_ref, k_ref, v_ref, lam_ref, g_ref, o_ref, m_ref, l_ref, acc_ref,
                      *, blk, hd, vd, hb, lam_init):
    hg = pl.program_id(1)
    qi = pl.program_id(2)
    m_ref[...] = jnp.full(m_ref.shape, NEG_INF, F32)
    l_ref[...] = jnp.zeros(l_ref.shape, F32)
    acc_ref[...] = jnp.zeros(acc_ref.shape, F32)
    col = lax.broadcasted_iota(jnp.int32, (1, blk), 1)
    chains = lambda ref: [ref.at[i] for i in range(2 * hb)]
    slopes2 = [slope_ref[hg * hb + h] * LOG2E for h in range(hb)]

    def step(j, bias_of, mask):
        st = pl.multiple_of(j * blk, blk)
        ss, vs = [], []
        for h in range(hb):
            bias = bias_of(slopes2[h])
            v = v_ref[pl.ds(st, blk), h * vd:(h + 1) * vd]
            for m in range(2):
                sl = slice((2 * h + m) * hd, (2 * h + m + 1) * hd)
                s = _dot_nt(q_ref[:, sl], k_ref[pl.ds(st, blk), sl]) + bias
                ss.append(s if mask is None else jnp.where(mask, s, NEG_INF))
                vs.append(v)
        _da_steps(ss, vs, chains(m_ref), chains(l_ref), chains(acc_ref))

    def body(j, c):
        rel = (col + (j - qi) * blk).astype(F32)
        step(j, lambda slope2: slope2 * rel, None)
        return c

    lax.fori_loop(0, qi, body, 0)
    step(qi, lambda slope2: _da_same_block_bias(blk, blk, slope2), _da_chunk_mask(blk, blk))
    lam = _da_lambda(lam_ref, lam_init)
    for h in range(hb):
        o_ref[:, h * vd:(h + 1) * vd] = _da_finish(
            acc_ref[2 * h], l_ref[2 * h], acc_ref[2 * h + 1], l_ref[2 * h + 1],
            lam, g_ref[...], lam_init).astype(o_ref.dtype)


def _da_prompt(q, k, v, slopes, lam_vecs, gain, batch, seq, heads, hd, vd, lam_init):
    blk = min(512, seq)
    hb = math.gcd(heads, 2)
    assert seq % blk == 0 and blk % CHUNK == 0
    nq = seq // blk
    blocks = hb * (blk * 2 * hd * 2 + blk * vd * 2 + seq * 2 * hd * 2 + seq * vd * 2)
    scratch = hb * (2 * 2 * blk * LANES * 4 + 2 * blk * vd * 4)
    return pl.pallas_call(
        functools.partial(_da_prompt_kernel, blk=blk, hd=hd, vd=vd, hb=hb, lam_init=lam_init),
        grid=(batch, heads // hb, nq),
        in_specs=[pl.BlockSpec(memory_space=pltpu.SMEM),
                  pl.BlockSpec((blk, hb * 2 * hd), lambda b, h, i: (b * nq + i, h)),
                  pl.BlockSpec((seq, hb * 2 * hd), lambda b, h, i: (b, h)),
                  pl.BlockSpec((seq, hb * vd), lambda b, h, i: (b, h)),
                  pl.BlockSpec((4, hd), lambda b, h, i: (0, 0)),
                  pl.BlockSpec((1, vd), lambda b, h, i: (0, 0))],
        out_specs=pl.BlockSpec((blk, hb * vd), lambda b, h, i: (b * nq + i, h)),
        out_shape=jax.ShapeDtypeStruct(v.shape, BF16),
        scratch_shapes=[pltpu.VMEM((2 * hb, blk, LANES), F32), pltpu.VMEM((2 * hb, blk, LANES), F32),
                        pltpu.VMEM((2 * hb, blk, vd), F32)],
        compiler_params=pltpu.CompilerParams(
            dimension_semantics=("arbitrary", "arbitrary", "arbitrary"),
            vmem_limit_bytes=_vmem_limit(blocks, scratch)),
        name="da_prompt",
    )(slopes, q, k, v, lam_vecs, gain)


def _da_sample_kernel(q_ref, kn_ref, vn_ref, kc_ref, vc_ref, lam_ref, g_ref, o_ref,
                      m_ref, l_ref, acc_ref, *, heads, hd, vd, past, slopes, lam_init):
    kb = pl.program_id(1)
    nkb = pl.num_programs(1) - 1
    nq = q_ref.shape[0]
    bk = kc_ref.shape[0] // SUBLANES
    chains = 2 * heads
    refs = lambda ref: [ref.at[i] for i in range(chains)]

    @pl.when(kb == 0)
    def _():
        m_ref[...] = jnp.full(m_ref.shape, NEG_INF, F32)
        l_ref[...] = jnp.zeros(l_ref.shape, F32)
        acc_ref[...] = jnp.zeros(acc_ref.shape, F32)

    def step(k_of, v_of, bias_of):
        ss, vs = [], []
        for h in range(heads):
            v = v_of(h)
            bias = bias_of(slopes[h] * LOG2E)
            for m in range(2):
                i = 2 * h + m
                ss.append(_dot_nt(q_ref[:, i * hd:(i + 1) * hd], k_of(i)) + bias)
                vs.append(v)
        _da_steps(ss, vs, refs(m_ref), refs(l_ref), refs(acc_ref))

    @pl.when(kb < nkb)
    def _():
        col = lax.broadcasted_iota(jnp.int32, (1, bk), 1)
        rel = (col + (kb * bk - past)).astype(F32)
        step(lambda i: _tile_rows(kc_ref, i, bk).astype(BF16),
             lambda h: jnp.concatenate([_tile_rows(vc_ref, half * heads + h, bk) for half in range(vd // LANES)],
                                       axis=1).astype(BF16),
             lambda slope2: slope2 * rel)

    @pl.when(kb == nkb)
    def _():
        step(lambda i: kn_ref[:, i * hd:(i + 1) * hd], lambda h: vn_ref[:, h * vd:(h + 1) * vd],
             lambda slope2: _da_same_block_bias(nq, nq, slope2))
        lam = _da_lambda(lam_ref, lam_init)
        for h in range(heads):
            o_ref[:, h * vd:(h + 1) * vd] = _da_finish(
                acc_ref[2 * h], l_ref[2 * h], acc_ref[2 * h + 1], l_ref[2 * h + 1],
                lam, g_ref[...], lam_init).astype(o_ref.dtype)


def _da_sample(q, kn, vn, kc, vc, slopes, lam_vecs, gain, batch, nq, past, heads, hd, vd, lam_init):
    assert nq == CHUNK and past % CHUNK == 0
    assert hd == LANES and 2 * heads == SUBLANES and heads * (vd // LANES) == SUBLANES
    qk_w = heads * 2 * hd
    v_w = heads * vd
    bk = min(512, past)
    assert past % bk == 0
    nkb = past // bk
    blocks = 2 * nq * qk_w * 2 + 2 * nq * v_w * 2 + bk * (qk_w + v_w) * 4
    scratch = 2 * 2 * heads * nq * LANES * 4 + 2 * heads * nq * vd * 4
    cache_idx = lambda b, j: (b * nkb + jnp.minimum(j, nkb - 1), 0)
    return pl.pallas_call(
        functools.partial(_da_sample_kernel, heads=heads, hd=hd, vd=vd, past=past, slopes=slopes,
                          lam_init=lam_init),
        grid=(batch, nkb + 1),
        in_specs=[pl.BlockSpec((nq, qk_w), lambda b, j: (b, 0)),
                  pl.BlockSpec((nq, qk_w), lambda b, j: (b, 0)),
                  pl.BlockSpec((nq, v_w), lambda b, j: (b, 0)),
                  pl.BlockSpec((bk * SUBLANES, LANES), cache_idx),
                  pl.BlockSpec((bk * SUBLANES, LANES), cache_idx),
                  pl.BlockSpec((4, hd), lambda b, j: (0, 0)),
                  pl.BlockSpec((1, vd), lambda b, j: (0, 0))],
        out_specs=pl.BlockSpec((nq, v_w), lambda b, j: (b, 0)),
        out_shape=jax.ShapeDtypeStruct(vn.shape, BF16),
        scratch_shapes=[pltpu.VMEM((2 * heads, nq, LANES), F32), pltpu.VMEM((2 * heads, nq, LANES), F32),
                        pltpu.VMEM((2 * heads, nq, vd), F32)],
        compiler_params=pltpu.CompilerParams(
            dimension_semantics=("arbitrary", "arbitrary"),
            vmem_limit_bytes=_vmem_limit(blocks, scratch)),
        name="da_sample",
    )(q, kn, vn, kc, vc, lam_vecs, gain)


def _route(logits, n_groups, per_group):
    n_exp = n_groups * per_group
    lane = lax.broadcasted_iota(jnp.int32, logits.shape, 1).astype(F32)
    big = jnp.float32(3e38)

    def first_argmax(x):
        top = jnp.max(x, axis=1, keepdims=True)
        idx = jnp.min(jnp.where(x == top, lane, big), axis=1, keepdims=True)
        return top, idx

    is_coarse = (lane >= n_exp) & (lane < n_exp + n_groups)
    coarse = jnp.where(is_coarse, logits, -big)
    c_top, c_idx = first_argmax(coarse)
    g_gate = 1.0 / jnp.sum(jnp.where(is_coarse, jnp.exp(logits - c_top), 0.0), axis=1, keepdims=True)
    g_first = (c_idx - n_exp) * per_group
    in_group = (lane >= g_first) & (lane < g_first + per_group)
    fine = jnp.where(in_group, logits, -big)
    v1, i1 = first_argmax(fine)
    v2, i2 = first_argmax(jnp.where(lane == i1, -big, fine))
    e2 = jnp.exp(v2 - v1)
    w1 = g_gate / (1.0 + e2)
    w2 = g_gate * e2 / (1.0 + e2)
    comb = jnp.where(lane == i1, w1, jnp.where(lane == i2, w2, 0.0))
    return jnp.where(lane == n_exp, c_idx - n_exp, comb)


def _post_attn_kernel(sb_ref, da_ref, x_ref, wsb_ref, wda_ref, g_ref, b_ref, wr_ref, br_ref,
                      h1_ref, comb_ref, *, alpha, n_groups, per_group):
    mixed = _dot(sb_ref[...], wsb_ref[...]) + _dot(da_ref[...], wda_ref[...])
    h1 = _layer_norm(alpha * x_ref[...] + mixed, g_ref[...], b_ref[...])
    h1_ref[...] = h1
    h_hi, h_lo = _split_bf16(h1)
    both = _dot(h_hi, wr_ref[...])
    logits = both[:, :LANES] + both[:, LANES:] + _dot(h_lo, wr_ref[:, :LANES]) + br_ref[...]
    comb_ref[...] = _route(logits, n_groups, per_group)


def _post_attn(sb_o, da_o, x, w_out, ln_g, ln_b, w_route, b_route, alpha, n_groups, per_group):
    t, d = x.shape
    w_sb = sb_o.shape[1]
    w_da = da_o.shape[1]
    assert w_sb == w_da
    tm = min(256, t)
    assert t % tm == 0
    blocks = (tm * (w_sb + w_da) * 2 + tm * d * 4 * 2 + (w_sb + w_da) * d * 2
              + d * 2 * LANES * 2 + tm * LANES * 4)
    row = lambda i: (i, 0)
    fixed = lambda i: (0, 0)
    return pl.pallas_call(
        functools.partial(_post_attn_kernel, alpha=alpha, n_groups=n_groups, per_group=per_group),
        grid=(t // tm,),
        in_specs=[pl.BlockSpec((tm, w_sb), row), pl.BlockSpec((tm, w_da), row), pl.BlockSpec((tm, d), row),
                  pl.BlockSpec((w_sb, d), fixed), pl.BlockSpec((w_da, d), lambda i: (1, 0)),
                  pl.BlockSpec((1, d), fixed), pl.BlockSpec((1, d), fixed),
                  pl.BlockSpec((d, 2 * LANES), fixed), pl.BlockSpec((1, LANES), fixed)],
        out_specs=[pl.BlockSpec((tm, d), row), pl.BlockSpec((tm, LANES), row)],
        out_shape=[jax.ShapeDtypeStruct((t, d), F32), jax.ShapeDtypeStruct((t, LANES), F32)],
        compiler_params=pltpu.CompilerParams(
            dimension_semantics=("arbitrary",), vmem_limit_bytes=_vmem_limit(blocks)),
        name="post_attn",
    )(sb_o, da_o, x, w_out, w_out, ln_g, ln_b, w_route, b_route)


def _dispatch_plan(comb, n_exp, n_groups, tm):
    t = comb.shape[0]
    per_group = n_exp // n_groups
    n_tiles = t // tm + n_groups
    group = comb[:, n_exp].astype(jnp.int32)
    active = comb[:, :n_exp] != 0.0
    active_bits = jnp.sum(active.astype(jnp.int32) << jnp.arange(n_exp, dtype=jnp.int32)[None, :], axis=1)
    counts = jnp.sum((group[:, None] == jnp.arange(n_groups)[None, :]).astype(jnp.int32), axis=0)
    tiles_g = (counts + tm - 1) // tm
    tile_end = jnp.cumsum(tiles_g)
    tile_start = tile_end - tiles_g
    count_start = jnp.cumsum(counts) - counts
    tile_ids = jnp.arange(n_tiles, dtype=jnp.int32)
    tile_group = jnp.minimum(jnp.sum((tile_ids[:, None] >= tile_end[None, :]).astype(jnp.int32), axis=1),
                             n_groups - 1)
    first_row = (tile_ids - tile_start[tile_group]) * tm
    tile_valid = jnp.clip(counts[tile_group] - first_row, 0, tm)
    order = jnp.argsort(group * (1 << n_exp) + active_bits, stable=True).astype(jnp.int32)
    slot_group = jnp.repeat(tile_group, tm)
    rank = jnp.repeat(first_row, tm) + jnp.tile(jnp.arange(tm, dtype=jnp.int32), n_tiles)
    rank = jnp.minimum(rank, counts[slot_group] - 1)
    src = order[jnp.clip(count_start[slot_group] + rank, 0, t - 1)]
    tile_active = jnp.any(active[src].reshape(n_tiles, tm, n_exp), axis=1)
    step_expert = (tile_group[:, None] * per_group + jnp.arange(per_group)[None, :]).reshape(-1)
    need = jnp.take_along_axis(tile_active.reshape(-1), step_expert + jnp.repeat(tile_ids, per_group) * n_exp, 0)
    need = jnp.logical_and(need, jnp.repeat(tile_ids < tile_end[-1], per_group))
    steps = jnp.arange(n_tiles * per_group, dtype=jnp.int32)
    last_needed = lax.cummax(jnp.where(need, steps, -1), axis=0)
    resident = step_expert[jnp.maximum(last_needed, 0)]
    return (src, tile_group.astype(jnp.int32), tile_valid.astype(jnp.int32), tile_end[-1:].astype(jnp.int32),
            need.astype(jnp.int32), resident.astype(jnp.int32))


def _moe_kernel(src_ref, grp_ref, valid_ref, used_ref, need_ref, resident_ref,
                h1_hbm, comb_ref, wg_ref, wu_ref, wd_ref, g_ref, b_ref,
                y_hbm, xbuf, xb_ref, acc_ref, obuf, sem_in, sem_out, *, alpha, tm, per_group):
    i = pl.program_id(0)
    j = pl.program_id(1)
    n_used = used_ref[0]
    slot = lax.rem(i, 2)
    live = i < n_used

    def row_in(tile, r, buf_slot):
        return pltpu.make_async_copy(h1_hbm.at[pl.ds(src_ref[tile * tm + r], 1), :],
                                     xbuf.at[buf_slot, pl.ds(r, 1), :], sem_in.at[buf_slot])

    def row_out(tile, r):
        return pltpu.make_async_copy(obuf.at[pl.ds(r, 1), :],
                                     y_hbm.at[pl.ds(src_ref[tile * tm + r], 1), :], sem_out)

    def gather(tile, buf_slot):
        def issue(r, c):
            row_in(tile, r, buf_slot).start()
            return c
        lax.fori_loop(0, tm, issue, 0, unroll=8)

    def drain_scatter(tile):
        def wait(r, c):
            row_out(tile, 0).wait()
            return c
        lax.fori_loop(0, valid_ref[tile], wait, 0)

    @pl.when(jnp.logical_and(live, j == 0))
    def _():
        @pl.when(i == 0)
        def _():
            gather(0, 0)
        pltpu.make_async_copy(h1_hbm.at[pl.ds(0, tm), :], xbuf.at[slot], sem_in.at[slot]).wait()

        @pl.when(i + 1 < n_used)
        def _():
            gather(i + 1, 1 - slot)
        xb_ref[...] = xbuf[slot].astype(BF16)
        acc_ref[...] = jnp.zeros(acc_ref.shape, F32)

    @pl.when(jnp.logical_and(live, need_ref[i * per_group + j] > 0))
    def _():
        xb = xb_ref[...]
        gate = _dot(xb, wg_ref[...])
        up = _dot(xb, wu_ref[...])
        hidden = (gate / (1.0 + jnp.exp(-gate))) * up
        lane = lax.broadcasted_iota(jnp.int32, comb_ref.shape, 1)
        expert = grp_ref[i] * per_group + j
        weight = jnp.sum(jnp.where(lane == expert, comb_ref[...], 0.0), axis=1, keepdims=True)
        acc_ref[...] += weight * _dot(hidden.astype(BF16), wd_ref[...])

    @pl.when(jnp.logical_and(live, j == per_group - 1))
    def _():
        @pl.when(i > 0)
        def _():
            drain_scatter(i - 1)
        obuf[...] = _layer_norm(alpha * xbuf[slot] + acc_ref[...], g_ref[...], b_ref[...])

        def issue(r, c):
            row_out(i, r).start()
            return c
        lax.fori_loop(0, valid_ref[i], issue, 0)

        @pl.when(i == n_used - 1)
        def _():
            drain_scatter(i)


def _moe(h1, comb, w_gate, w_up, w_down, ln_g, ln_b, alpha, n_groups):
    t, d = h1.shape
    n_exp, _, d_e = w_gate.shape
    per_group = n_exp // n_groups
    tm = min(512, t)
    assert t % tm == 0
    src, tile_group, tile_valid, n_used, need, resident = _dispatch_plan(comb, n_exp, n_groups, tm)
    n_tiles = tile_group.shape[0]
    comb_sorted = comb[src]
    blocks = tm * LANES * 4 + 3 * d * d_e * 2
    scratch = 2 * tm * d * 4 + tm * d * 2 + 2 * tm * d * 4
    expert_idx = lambda i, j, src, grp, valid, used, need, res: (res[i * per_group + j], 0, 0)
    fixed = lambda i, j, *prefetch: (0, 0)
    return pl.pallas_call(
        functools.partial(_moe_kernel, alpha=alpha, tm=tm, per_group=per_group),
        grid_spec=pltpu.PrefetchScalarGridSpec(
            num_scalar_prefetch=6,
            grid=(n_tiles, per_group),
            in_specs=[pl.BlockSpec(memory_space=pl.ANY),
                      pl.BlockSpec((tm, LANES), lambda i, j, *prefetch: (i, 0)),
                      pl.BlockSpec((None, d, d_e), expert_idx),
                      pl.BlockSpec((None, d, d_e), expert_idx),
                      pl.BlockSpec((None, d_e, d), expert_idx),
                      pl.BlockSpec((1, d), fixed), pl.BlockSpec((1, d), fixed)],
            out_specs=pl.BlockSpec(memory_space=pl.ANY),
            scratch_shapes=[pltpu.VMEM((2, tm, d), F32), pltpu.VMEM((tm, d), BF16), pltpu.VMEM((tm, d), F32),
                            pltpu.VMEM((tm, d), F32), pltpu.SemaphoreType.DMA((2,)), pltpu.SemaphoreType.DMA]),
        out_shape=jax.ShapeDtypeStruct((t, d), F32),
        compiler_params=pltpu.CompilerParams(
            dimension_semantics=("arbitrary", "arbitrary"),
            vmem_limit_bytes=_vmem_limit(blocks, scratch)),
        name="moe",
    )(src, tile_group, tile_valid, n_used, need, resident, h1, comb_sorted, w_gate, w_up, w_down, ln_g, ln_b)


def kernel(x_prompt, x_sample, cache_sb_k, cache_sb_v, cache_da_k, cache_da_v, w_in, w_out, lambda_q1, lambda_k1, lambda_q2, lambda_k2, subln_g, ln1_g, ln1_b, w_coarse, b_coarse, w_fine, b_fine, w_gate, w_up, w_down, ln2_g, ln2_b):
    depth, dec_batch, past, sb_heads, sb_hd = cache_sb_k.shape
    _, _, _, da_heads, _, da_hd = cache_da_k.shape
    da_vd = cache_da_v.shape[-1]
    batch, seq, d_model = x_prompt.shape
    dec_seq = x_sample.shape[1]
    n_groups, per_group = w_fine.shape[2], w_fine.shape[3]
    n_exp = n_groups * per_group
    assert n_exp + n_groups <= LANES
    sb_w = sb_heads * sb_hd
    assert sb_w == da_heads * 2 * da_hd == da_heads * da_vd == SUBLANES * LANES, \
        "projection column groups are addressed as equal-width blocks of one token tile"
    alpha = (2 * depth) ** 0.25
    slopes = tuple(2.0 ** (-8.0 * (h + 1) / da_heads) for h in range(da_heads))
    slopes_arr = jnp.asarray(slopes, F32)

    hp = x_prompt.reshape(batch * seq, d_model)
    hs = x_sample.reshape(dec_batch * dec_seq, d_model)
    rows_p, rows_s = [], []
    for l in range(depth):
        lam_init = 0.8 - 0.6 * math.exp(-0.3 * l)
        w_in_b = w_in[l].astype(BF16)
        w_out_b = w_out[l].astype(BF16)
        lam_vecs = jnp.stack([lambda_q1[l], lambda_k1[l], lambda_q2[l], lambda_k2[l]]).astype(F32)
        gain = subln_g[l].reshape(1, da_vd)
        w_route = jnp.concatenate(
            [w_fine[l].reshape(d_model, n_exp), w_coarse[l],
             jnp.zeros((d_model, LANES - n_exp - n_groups), F32)], axis=1)
        w_route_b = jnp.concatenate(_split_bf16(w_route), axis=1)
        b_route = jnp.concatenate(
            [b_fine[l].reshape(n_exp), b_coarse[l], jnp.zeros((LANES - n_exp - n_groups,), F32)]).reshape(1, LANES)
        wg_b, wu_b, wd_b = w_gate[l].astype(BF16), w_up[l].astype(BF16), w_down[l].astype(BF16)
        g1, b1 = ln1_g[l].reshape(1, d_model), ln1_b[l].reshape(1, d_model)
        g2, b2 = ln2_g[l].reshape(1, d_model), ln2_b[l].reshape(1, d_model)

        halves = da_vd // LANES
        dv_slots = tuple(h * halves + half for half in range(halves) for h in range(da_heads))
        plain_slots = tuple(range(SUBLANES))

        sb_scale = sb_hd ** -0.5 * LOG2E
        da_scale = da_hd ** -0.5 * LOG2E

        def project(x, tag):
            sq, sk_b, sv_b, sk, sv = _proj(
                x, w_in_b, 0, ((sb_scale, None), (None, plain_slots), (None, plain_slots)), "proj_sb_" + tag)
            dq, dk_b, dv_b, dk, dv = _proj(
                x, w_in_b, 3, ((da_scale, None), (None, plain_slots), (None, dv_slots)), "proj_da_" + tag)
            return (sq, sk_b, sv_b, dq, dk_b, dv_b), (sk, sv, dk, dv)

        def dv_tiles_to_rows(tiles, lead):
            return tiles.reshape(lead + (halves, da_heads, LANES)).swapaxes(-3, -2).reshape(lead + (da_heads, da_vd))

        def dv_rows_to_tiles(rows_, n):
            return rows_.reshape(n, da_heads, halves, LANES).swapaxes(1, 2).reshape(n * SUBLANES, LANES)

        (sq, sk, sv, dq, dk, dv), new_p = project(hp, "p")
        sb_o = _sb_prompt(sq, sk, sv, batch, seq, sb_heads, sb_hd)
        da_o = _da_prompt(dq, dk, dv, slopes_arr, lam_vecs, gain, batch, seq, da_heads, da_hd, da_vd, lam_init)
        h1, comb = _post_attn(sb_o, da_o, hp, w_out_b, g1, b1, w_route_b, b_route, alpha, n_groups, per_group)
        hp = _moe(h1, comb, wg_b, wu_b, wd_b, g2, b2, alpha, n_groups)

        (sq, sk, sv, dq, dk, dv), new_s = project(hs, "s")
        rows = dec_batch * past
        sb_o = _sb_sample(sq, sk, sv, cache_sb_k[l].reshape(rows * SUBLANES, LANES),
                          cache_sb_v[l].reshape(rows * SUBLANES, LANES),
                          dec_batch, dec_seq, past, sb_heads, sb_hd)
        da_o = _da_sample(dq, dk, dv, cache_da_k[l].reshape(rows * SUBLANES, LANES),
                          dv_rows_to_tiles(cache_da_v[l], rows),
                          slopes, lam_vecs, gain, dec_batch, dec_seq, past, da_heads, da_hd, da_vd, lam_init)
        h1, comb = _post_attn(sb_o, da_o, hs, w_out_b, g1, b1, w_route_b, b_route, alpha, n_groups, per_group)
        hs = _moe(h1, comb, wg_b, wu_b, wd_b, g2, b2, alpha, n_groups)
        for new, lead, rows_out in ((new_p, (batch, seq), rows_p), (new_s, (dec_batch, dec_seq), rows_s)):
            rows_out.append((new[0].reshape(lead + (sb_heads, sb_hd)), new[1].reshape(lead + (sb_heads, sb_hd)),
                             new[2].reshape(lead + (da_heads, 2, da_hd)), dv_tiles_to_rows(new[3], lead)))

    def stack(rows, i):
        return rows[0][i][None] if len(rows) == 1 else jnp.stack([r[i] for r in rows], axis=0)

    return (hp.reshape(batch, seq, d_model), hs.reshape(dec_batch, dec_seq, d_model),
            *(stack(rows_p, i) for i in range(4)), *(stack(rows_s, i) for i in range(4)))
```

```python
import functools
import math

import jax
import jax.numpy as jnp
from jax import lax
from jax.experimental import pallas as pl
from jax.experimental.pallas import tpu as pltpu

F32 = jnp.float32
BF16 = jnp.bfloat16

CHUNK = 64
LN_EPS = 1e-5
RMS_EPS = 1e-5
NEG_INF = -1e30
LOG2E = 1.4426950408889634
LANES = 128
SUBLANES = 8
VMEM_CAP = 64 * 2**20
SB_SKIP_LOG2 = 150.0


def _vmem_limit(block_bytes, scratch_bytes=0):
    est = 2 * block_bytes + scratch_bytes + 16 * 2**20
    return int(min(est, VMEM_CAP - 6 * 2**20))


def _dot(a, b):
    return jnp.dot(a, b, preferred_element_type=F32)


def _dot_nt(a, b):
    return lax.dot_general(a, b, (((1,), (1,)), ((), ())), preferred_element_type=F32)


def _widen(x, width):
    if width < LANES:
        return x[:, :width]
    return x if width == LANES else jnp.concatenate([x] * (width // LANES), axis=1)


def _split_bf16(x):
    hi = x.astype(BF16)
    return hi, (x - hi.astype(F32)).astype(BF16)


def _layer_norm(y, g, b):
    mu = jnp.mean(y, axis=1, keepdims=True)
    yc = y - mu
    var = jnp.mean(yc * yc, axis=1, keepdims=True)
    return yc * lax.rsqrt(var + LN_EPS) * g + b


def _tile_rows(ref, slot, n):
    return ref[pl.ds(slot, n, stride=SUBLANES), :]


def _proj_kernel(x_ref, w_ref, *out_refs, groups):
    xb = x_ref[...].astype(BF16)
    width = SUBLANES * LANES
    o32_refs = iter(out_refs[len(groups):])
    for g, (scale, slots) in enumerate(groups):
        acc = _dot(xb, w_ref[:, g * width:(g + 1) * width])
        out_refs[g][...] = (acc if scale is None else acc * scale).astype(BF16)
        if slots:
            o32_ref = next(o32_refs)
            for j, cb in enumerate(slots):
                o32_ref[pl.ds(j, acc.shape[0], stride=SUBLANES), :] = acc[:, cb * LANES:(cb + 1) * LANES]


def _proj(x, w, first_group, groups, name):
    t, d = x.shape
    width = SUBLANES * LANES
    n = len(groups)
    assert first_group % n == 0
    tm = min(512, t)
    assert t % tm == 0
    n32 = sum(1 for _, slots in groups if slots)
    blk = tm * d * 4 + d * n * width * 2 + tm * width * (2 * n + 4 * n32)
    out_specs = ([pl.BlockSpec((tm, width), lambda i: (i, 0))] * n
                 + [pl.BlockSpec((tm * SUBLANES, LANES), lambda i: (i, 0))] * n32)
    out_shape = ([jax.ShapeDtypeStruct((t, width), BF16)] * n
                 + [jax.ShapeDtypeStruct((t * SUBLANES, LANES), F32)] * n32)
    return pl.pallas_call(
        functools.partial(_proj_kernel, groups=groups),
        grid=(t // tm,),
        in_specs=[pl.BlockSpec((tm, d), lambda i: (i, 0)),
                  pl.BlockSpec((d, n * width), lambda i: (0, first_group // n))],
        out_specs=out_specs,
        out_shape=out_shape,
        compiler_params=pltpu.CompilerParams(
            dimension_semantics=("arbitrary",), vmem_limit_bytes=_vmem_limit(blk)),
        name=name,
    )(x, w)


def _inclusive_suffix_matrix(n):
    r = lax.broadcasted_iota(jnp.int32, (2 * n, n), 0)
    c = lax.broadcasted_iota(jnp.int32, (2 * n, n), 1)
    return (jnp.where(r >= n, r - n, r) >= c).astype(BF16)


def _strict_causal_mask(nq, nk):
    r = lax.broadcasted_iota(jnp.int32, (nq, nk), 0)
    c = lax.broadcasted_iota(jnp.int32, (nq, nk), 1)
    return c < r


def _sb_blocks(qs, ks, vs, carries, accs, umat, mask):
    n = len(qs)
    nq = qs[0].shape[0]
    zs = [_dot_nt(qs[i], ks[i]) for i in range(n)]
    splits = []
    for z in zs:
        neg_abs = lax.bitcast_convert_type(
            lax.bitcast_convert_type(z, jnp.uint32) | jnp.uint32(0x80000000), F32)
        sp = jnp.maximum(z, 0.0) + jnp.log(1.0 + jnp.exp2(neg_abs)) * LOG2E
        if mask is not None:
            sp = jnp.where(mask, sp, 0.0)
        splits.append(jnp.concatenate(_split_bf16(sp), axis=1))
    incl_all = _dot(jnp.concatenate(splits, axis=0) if n > 1 else splits[0], umat)
    out_c, out_a = [], []
    for i in range(n):
        incl = incl_all[i * nq:(i + 1) * nq]
        w = jnp.exp2(zs[i] - incl - carries[i])
        if mask is not None:
            w = jnp.where(mask, w, 0.0)
        out_a.append(accs[i] + _dot(w.astype(BF16), vs[i]))
        out_c.append(carries[i] + incl[:, :1])
    return out_c, out_a


def _all_exhausted(carries):
    lowest = functools.reduce(jnp.minimum, carries)
    return jnp.min(lowest) >= SB_SKIP_LOG2


def _sb_prompt_kernel(q_ref, k_ref, v_ref, o_ref, carry_ref, acc_ref, *, blk, hd):
    qi = pl.program_id(2)
    nh = q_ref.shape[1] // hd
    umat = _inclusive_suffix_matrix(blk)
    sls = [slice(h * hd, (h + 1) * hd) for h in range(nh)]

    def run(start, carries, accs, mask):
        cs, as_ = _sb_blocks([q_ref[:, sl] for sl in sls], [k_ref[pl.ds(start, blk), sl] for sl in sls],
                             [v_ref[pl.ds(start, blk), sl] for sl in sls], carries, accs, umat, mask)
        for h in range(nh):
            carry_ref[h] = cs[h]
            acc_ref[:, sls[h]] = as_[h]
        return cs

    run(pl.multiple_of(qi * blk, blk), [jnp.zeros((blk, 1), F32)] * nh, [jnp.zeros((blk, hd), F32)] * nh,
        _strict_causal_mask(blk, blk))

    def cond(state):
        j, go = state
        return jnp.logical_and(j >= 0, go > 0)

    def body(state):
        j, _ = state
        cs = run(pl.multiple_of(j * blk, blk), [carry_ref[h] for h in range(nh)],
                 [acc_ref[:, sl] for sl in sls], None)
        return j - 1, jnp.logical_not(_all_exhausted(cs)).astype(jnp.int32)

    lax.while_loop(cond, body, (qi - 1, jnp.int32(1)))
    o_ref[...] = acc_ref[...].astype(o_ref.dtype)


def _sb_prompt(q, k, v, batch, seq, heads, hd):
    blk = min(256, seq)
    hb = math.gcd(heads, 4)
    assert seq % blk == 0
    nq = seq // blk
    w = hb * hd
    blocks = blk * w * 2 * 2 + 2 * seq * w * 2
    scratch = hb * blk * LANES * 4 + blk * w * 4
    return pl.pallas_call(
        functools.partial(_sb_prompt_kernel, blk=blk, hd=hd),
        grid=(batch, heads // hb, nq),
        in_specs=[pl.BlockSpec((blk, w), lambda b, h, i: (b * nq + i, h)),
                  pl.BlockSpec((seq, w), lambda b, h, i: (b, h)),
                  pl.BlockSpec((seq, w), lambda b, h, i: (b, h))],
        out_specs=pl.BlockSpec((blk, w), lambda b, h, i: (b * nq + i, h)),
        out_shape=jax.ShapeDtypeStruct(q.shape, BF16),
        scratch_shapes=[pltpu.VMEM((hb, blk, 1), F32), pltpu.VMEM((blk, w), F32)],
        compiler_params=pltpu.CompilerParams(
            dimension_semantics=("arbitrary", "arbitrary", "arbitrary"),
            vmem_limit_bytes=_vmem_limit(blocks, scratch)),
        name="sb_prompt",
    )(q, k, v)


def _sb_sample_kernel(q_ref, kn_ref, vn_ref, kc_ref, vc_ref, o_ref, carry_ref, acc_ref, u_ref, go_ref,
                      *, heads, hd):
    kb = pl.program_id(1)
    nq = q_ref.shape[0]
    bk = u_ref.shape[1]
    sls = [slice(h * hd, (h + 1) * hd) for h in range(heads)]

    def run(ks, vs, carries, accs, umat, mask):
        cs, as_ = _sb_blocks([q_ref[:, sl] for sl in sls], ks, vs, carries, accs, umat, mask)
        for h in range(heads):
            carry_ref[h] = cs[h]
            acc_ref[:, sls[h]] = as_[h]
        go_ref[0] = jnp.logical_not(_all_exhausted(cs)).astype(jnp.int32)

    @pl.when(kb == 0)
    def _():
        u_ref[...] = _inclusive_suffix_matrix(bk)
        run([kn_ref[:, sl] for sl in sls], [vn_ref[:, sl] for sl in sls],
            [jnp.zeros((nq, 1), F32)] * heads, [jnp.zeros((nq, hd), F32)] * heads,
            _inclusive_suffix_matrix(nq), _strict_causal_mask(nq, nq))

    @pl.when(jnp.logical_and(kb > 0, go_ref[0] > 0))
    def _():
        run([_tile_rows(kc_ref, h, bk).astype(BF16) for h in range(heads)],
            [_tile_rows(vc_ref, h, bk).astype(BF16) for h in range(heads)],
            [carry_ref[h] for h in range(heads)], [acc_ref[:, sl] for sl in sls], u_ref[...], None)

    @pl.when(kb == pl.num_programs(1) - 1)
    def _():
        o_ref[...] = acc_ref[...].astype(o_ref.dtype)


def _sb_sample(q, kn, vn, kc, vc, batch, nq, past, heads, hd):
    assert heads == SUBLANES and hd == LANES
    width = heads * hd
    bk = min(512, past)
    assert past % bk == 0
    nkb = past // bk
    new_spec = pl.BlockSpec((nq, width), lambda b, j: (b, 0))
    cache_spec = pl.BlockSpec((bk * SUBLANES, LANES), lambda b, j: (b * nkb + nkb - jnp.maximum(j, 1), 0))
    blocks = 4 * nq * width * 2 + 2 * bk * width * 4
    scratch = heads * nq * LANES * 4 + nq * width * 4 + 2 * bk * bk * 2
    return pl.pallas_call(
        functools.partial(_sb_sample_kernel, heads=heads, hd=hd),
        grid=(batch, nkb + 1),
        in_specs=[new_spec, new_spec, new_spec, cache_spec, cache_spec],
        out_specs=new_spec,
        out_shape=jax.ShapeDtypeStruct(q.shape, BF16),
        scratch_shapes=[pltpu.VMEM((heads, nq, 1), F32), pltpu.VMEM((nq, width), F32),
                        pltpu.VMEM((2 * bk, bk), BF16), pltpu.SMEM((1,), jnp.int32)],
        compiler_params=pltpu.CompilerParams(
            dimension_semantics=("arbitrary", "arbitrary"),
            vmem_limit_bytes=_vmem_limit(blocks, scratch)),
        name="sb_sample",
    )(q, kn, vn, kc, vc)


def _da_steps(ss, vs, m_refs, l_refs, acc_refs):
    n = len(ss)
    nk = ss[0].shape[1]
    m_olds = [r[...] for r in m_refs]
    m_news = [jnp.maximum(m_olds[i], jnp.max(ss[i], axis=1, keepdims=True)) for i in range(n)]
    ps = [jnp.exp2(ss[i] - _widen(m_news[i], nk)) for i in range(n)]
    for i in range(n):
        alpha = jnp.exp2(m_olds[i] - m_news[i])
        l_refs[i][...] = alpha * l_refs[i][...] + jnp.sum(ps[i], axis=1, keepdims=True)
        acc_refs[i][...] = (_widen(alpha, vs[i].shape[1]) * acc_refs[i][...]
                            + _dot(ps[i].astype(BF16), vs[i]))
        m_refs[i][...] = m_news[i]


def _da_same_block_bias(nq, nk, slope2):
    r = lax.broadcasted_iota(jnp.int32, (nq, nk), 0)
    c = lax.broadcasted_iota(jnp.int32, (nq, nk), 1)
    return slope2 * (r - jnp.abs(r - c)).astype(F32)


def _da_chunk_mask(nq, nk):
    r = lax.broadcasted_iota(jnp.int32, (nq, nk), 0)
    c = lax.broadcasted_iota(jnp.int32, (nq, nk), 1)
    return (c // CHUNK) <= (r // CHUNK)


def _da_lambda(lam_ref, lam_init):
    lv = lam_ref[...]
    a = jnp.sum(lv[0:1] * lv[1:2], axis=1, keepdims=True)
    b = jnp.sum(lv[2:3] * lv[3:4], axis=1, keepdims=True)
    return jnp.exp(a) - jnp.exp(b) + lam_init


def _da_finish(acc0, l0, acc1, l1, lam, gain, lam_init):
    vd = acc0.shape[1]
    o = acc0 * _widen(1.0 / l0, vd) - lam * (acc1 * _widen(1.0 / l1, vd))
    y = o * lax.rsqrt(jnp.mean(o * o, axis=1, keepdims=True) + RMS_EPS)
    return (y * gain) * (1.0 - lam_init)


def _da_prompt_kernel(slope_ref, q_ref, k_ref, v_ref, lam_ref, g_ref, o_ref, m_ref, l_ref, acc_ref,
                      *, blk, hd, vd, hb, lam_init):
    hg = pl.program_id(1)
    qi = pl.program_id(2)
    m_ref[...] = jnp.full(m_ref.shape, NEG_INF, F32)
    l_ref[...] = jnp.zeros(l_ref.shape, F32)
    acc_ref[...] = jnp.zeros(acc_ref.shape, F32)
    col = lax.broadcasted_iota(jnp.int32, (1, blk), 1)
    chains = lambda ref: [ref.at[i] for i in range(2 * hb)]
    slopes2 = [slope_ref[hg * hb + h] * LOG2E for h in range(hb)]

    def step(j, bias_of, mask):
        st = pl.multiple_of(j * blk, blk)
        ss, vs = [], []
        for h in range(hb):
            bias = bias_of(slopes2[h])
            v = v_ref[pl.ds(st, blk), h * vd:(h + 1) * vd]
            for m in range(2):
                sl = slice((2 * h + m) * hd, (2 * h + m + 1) * hd)
                s = _dot_nt(q_ref[:, sl], k_ref[pl.ds(st, blk), sl]) + bias
                ss.append(s if mask is None else jnp.where(mask, s, NEG_INF))
                vs.append(v)
        _da_steps(ss, vs, chains(m_ref), chains(l_ref), chains(acc_ref))

    def body(j, c):
        rel = (col + (j - qi) * blk).astype(F32)
        step(j, lambda slope2: slope2 * rel, None)
        return c

    lax.fori_loop(0, qi, body, 0)
    step(qi, lambda slope2: _da_same_block_bias(blk, blk, slope2), _da_chunk_mask(blk, blk))
    lam = _da_lambda(lam_ref, lam_init)
    for h in range(hb):
        o_ref[:, h * vd:(h + 1) * vd] = _da_finish(
            acc_ref[2 * h], l_ref[2 * h], acc_ref[2 * h + 1], l_ref[2 * h + 1],
            lam, g_ref[...], lam_init).astype(o_ref.dtype)


def _da_prompt(q, k, v, slopes, lam_vecs, gain, batch, seq, heads, hd, vd, lam_init):
    blk = min(512, seq)
    hb = math.gcd(heads, 2)
    assert seq % blk == 0 and blk % CHUNK == 0
    nq = seq // blk
    blocks = hb * (blk * 2 * hd * 2 + blk * vd * 2 + seq * 2 * hd * 2 + seq * vd * 2)
    scratch = hb * (2 * 2 * blk * LANES * 4 + 2 * blk * vd * 4)
    return pl.pallas_call(
        functools.partial(_da_prompt_kernel, blk=blk, hd=hd, vd=vd, hb=hb, lam_init=lam_init),
        grid=(batch, heads // hb, nq),
        in_specs=[pl.BlockSpec(memory_space=pltpu.SMEM),
                  pl.BlockSpec((blk, hb * 2 * hd), lambda b, h, i: (b * nq + i, h)),
                  pl.BlockSpec((seq, hb * 2 * hd), lambda b, h, i: (b, h)),
                  pl.BlockSpec((seq, hb * vd), lambda b, h, i: (b, h)),
                  pl.BlockSpec((4, hd), lambda b, h, i: (0, 0)),
                  pl.BlockSpec((1, vd), lambda b, h, i: (0, 0))],
        out_specs=pl.BlockSpec((blk, hb * vd), lambda b, h, i: (b * nq + i, h)),
        out_shape=jax.ShapeDtypeStruct(v.shape, BF16),
        scratch_shapes=[pltpu.VMEM((2 * hb, blk, LANES), F32), pltpu.VMEM((2 * hb, blk, LANES), F32),
                        pltpu.VMEM((2 * hb, blk, vd), F32)],
        compiler_params=pltpu.CompilerParams(
            dimension_semantics=("arbitrary", "arbitrary", "arbitrary"),
            vmem_limit_bytes=_vmem_limit(blocks, scratch)),
        name="da_prompt",
    )(slopes, q, k, v, lam_vecs, gain)


def _da_sample_kernel(q_ref, kn_ref, vn_ref, kc_ref, vc_ref, lam_ref, g_ref, o_ref,
                      m_ref, l_ref, acc_ref, *, heads, hd, vd, past, slopes, lam_init):
    kb = pl.program_id(1)
    nkb = pl.num_programs(1) - 1
    nq = q_ref.shape[0]
    bk = kc_ref.shape[0] // SUBLANES
    chains = 2 * heads
    refs = lambda ref: [ref.at[i] for i in range(chains)]

    @pl.when(kb == 0)
    def _():
        m_ref[...] = jnp.full(m_ref.shape, NEG_INF, F32)
        l_ref[...] = jnp.zeros(l_ref.shape, F32)
        acc_ref[...] = jnp.zeros(acc_ref.shape, F32)

    def step(k_of, v_of, bias_of):
        ss, vs = [], []
        for h in range(heads):
            v = v_of(h)
            bias = bias_of(slopes[h] * LOG2E)
            for m in range(2):
                i = 2 * h + m
                ss.append(_dot_nt(q_ref[:, i * hd:(i + 1) * hd], k_of(i)) + bias)
                vs.append(v)
        _da_steps(ss, vs, refs(m_ref), refs(l_ref), refs(acc_ref))

    @pl.when(kb < nkb)
    def _():
        col = lax.broadcasted_iota(jnp.int32, (1, bk), 1)
        rel = (col + (kb * bk - past)).astype(F32)
        step(lambda i: _tile_rows(kc_ref, i, bk).astype(BF16),
             lambda h: jnp.concatenate([_tile_rows(vc_ref, half * heads + h, bk) for half in range(vd // LANES)],
                                       axis=1).astype(BF16),
             lambda slope2: slope2 * rel)

    @pl.when(kb == nkb)
    def _():
        step(lambda i: kn_ref[:, i * hd:(i + 1) * hd], lambda h: vn_ref[:, h * vd:(h + 1) * vd],
             lambda slope2: _da_same_block_bias(nq, nq, slope2))
        lam = _da_lambda(lam_ref, lam_init)
        for h in range(heads):
            o_ref[:, h * vd:(h + 1) * vd] = _da_finish(
                acc_ref[2 * h], l_ref[2 * h], acc_ref[2 * h + 1], l_ref[2 * h + 1],
                lam, g_ref[...], lam_init).astype(o_ref.dtype)


def _da_sample(q, kn, vn, kc, vc, slopes, lam_vecs, gain, batch, nq, past, heads, hd, vd, lam_init):
    assert nq == CHUNK and past % CHUNK == 0
    assert hd == LANES and 2 * heads == SUBLANES and heads * (vd // LANES) == SUBLANES
    qk_w = heads * 2 * hd
    v_w = heads * vd
    bk = min(512, past)
    assert past % bk == 0
    nkb = past // bk
    blocks = 2 * nq * qk_w * 2 + 2 * nq * v_w * 2 + bk * (qk_w + v_w) * 4
    scratch = 2 * 2 * heads * nq * LANES * 4 + 2 * heads * nq * vd * 4
    cache_idx = lambda b, j: (b * nkb + jnp.minimum(j, nkb - 1), 0)
    return pl.pallas_call(
        functools.partial(_da_sample_kernel, heads=heads, hd=hd, vd=vd, past=past, slopes=slopes,
                          lam_init=lam_init),
        grid=(batch, nkb + 1),
        in_specs=[pl.BlockSpec((nq, qk_w), lambda b, j: (b, 0)),
                  pl.BlockSpec((nq, qk_w), lambda b, j: (b, 0)),
                  pl.BlockSpec((nq, v_w), lambda b, j: (b, 0)),
                  pl.BlockSpec((bk * SUBLANES, LANES), cache_idx),
                  pl.BlockSpec((bk * SUBLANES, LANES), cache_idx),
                  pl.BlockSpec((4, hd), lambda b, j: (0, 0)),
                  pl.BlockSpec((1, vd), lambda b, j: (0, 0))],
        out_specs=pl.BlockSpec((nq, v_w), lambda b, j: (b, 0)),
        out_shape=jax.ShapeDtypeStruct(vn.shape, BF16),
        scratch_shapes=[pltpu.VMEM((2 * heads, nq, LANES), F32), pltpu.VMEM((2 * heads, nq, LANES), F32),
                        pltpu.VMEM((2 * heads, nq, vd), F32)],
        compiler_params=pltpu.CompilerParams(
            dimension_semantics=("arbitrary", "arbitrary"),
            vmem_limit_bytes=_vmem_limit(blocks, scratch)),
        name="da_sample",
    )(q, kn, vn, kc, vc, lam_vecs, gain)


def _route(logits, n_groups, per_group):
    n_exp = n_groups * per_group
    lane = lax.broadcasted_iota(jnp.int32, logits.shape, 1).astype(F32)
    big = jnp.float32(3e38)

    def first_argmax(x):
        top = jnp.max(x, axis=1, keepdims=True)
        idx = jnp.min(jnp.where(x == top, lane, big), axis=1, keepdims=True)
        return top, idx

    is_coarse = (lane >= n_exp) & (lane < n_exp + n_groups)
    coarse = jnp.where(is_coarse, logits, -big)
    c_top, c_idx = first_argmax(coarse)
    g_gate = 1.0 / jnp.sum(jnp.where(is_coarse, jnp.exp(logits - c_top), 0.0), axis=1, keepdims=True)
    g_first = (c_idx - n_exp) * per_group
    in_group = (lane >= g_first) & (lane < g_first + per_group)
    fine = jnp.where(in_group, logits, -big)
    v1, i1 = first_argmax(fine)
    v2, i2 = first_argmax(jnp.where(lane == i1, -big, fine))
    e2 = jnp.exp(v2 - v1)
    w1 = g_gate / (1.0 + e2)
    w2 = g_gate * e2 / (1.0 + e2)
    comb = jnp.where(lane == i1, w1, jnp.where(lane == i2, w2, 0.0))
    return jnp.where(lane == n_exp, c_idx - n_exp, comb)


def _post_attn_kernel(sb_ref, da_ref, x_ref, wsb_ref, wda_ref, g_ref, b_ref, wr_ref, br_ref,
                      h1_ref, comb_ref, *, alpha, n_groups, per_group):
    mixed = _dot(sb_ref[...], wsb_ref[...]) + _dot(da_ref[...], wda_ref[...])
    h1 = _layer_norm(alpha * x_ref[...] + mixed, g_ref[...], b_ref[...])
    h1_ref[...] = h1
    h_hi, h_lo = _split_bf16(h1)
    both = _dot(h_hi, wr_ref[...])
    logits = both[:, :LANES] + both[:, LANES:] + _dot(h_lo, wr_ref[:, :LANES]) + br_ref[...]
    comb_ref[...] = _route(logits, n_groups, per_group)


def _post_attn(sb_o, da_o, x, w_out, ln_g, ln_b, w_route, b_route, alpha, n_groups, per_group):
    t, d = x.shape
    w_sb = sb_o.shape[1]
    w_da = da_o.shape[1]
    assert w_sb == w_da
    tm = min(512, t)
    assert t % tm == 0
    blocks = (tm * (w_sb + w_da) * 2 + tm * d * 4 * 2 + (w_sb + w_da) * d * 2
              + d * 2 * LANES * 2 + tm * LANES * 4)
    row = lambda i: (i, 0)
    fixed = lambda i: (0, 0)
    return pl.pallas_call(
        functools.partial(_post_attn_kernel, alpha=alpha, n_groups=n_groups, per_group=per_group),
        grid=(t // tm,),
        in_specs=[pl.BlockSpec((tm, w_sb), row), pl.BlockSpec((tm, w_da), row), pl.BlockSpec((tm, d), row),
                  pl.BlockSpec((w_sb, d), fixed), pl.BlockSpec((w_da, d), lambda i: (1, 0)),
                  pl.BlockSpec((1, d), fixed), pl.BlockSpec((1, d), fixed),
                  pl.BlockSpec((d, 2 * LANES), fixed), pl.BlockSpec((1, LANES), fixed)],
        out_specs=[pl.BlockSpec((tm, d), row), pl.BlockSpec((tm, LANES), row)],
        out_shape=[jax.ShapeDtypeStruct((t, d), F32), jax.ShapeDtypeStruct((t, LANES), F32)],
        compiler_params=pltpu.CompilerParams(
            dimension_semantics=("arbitrary",), vmem_limit_bytes=_vmem_limit(blocks)),
        name="post_attn",
    )(sb_o, da_o, x, w_out, w_out, ln_g, ln_b, w_route, b_route)


def _dispatch_plan(comb, n_exp, n_groups, tm):
    t = comb.shape[0]
    per_group = n_exp // n_groups
    n_tiles = t // tm + n_groups
    group = comb[:, n_exp].astype(jnp.int32)
    active = comb[:, :n_exp] != 0.0
    active_bits = jnp.sum(active.astype(jnp.int32) << jnp.arange(n_exp, dtype=jnp.int32)[None, :], axis=1)
    counts = jnp.sum((group[:, None] == jnp.arange(n_groups)[None, :]).astype(jnp.int32), axis=0)
    tiles_g = (counts + tm - 1) // tm
    tile_end = jnp.cumsum(tiles_g)
    tile_start = tile_end - tiles_g
    count_start = jnp.cumsum(counts) - counts
    tile_ids = jnp.arange(n_tiles, dtype=jnp.int32)
    tile_group = jnp.minimum(jnp.sum((tile_ids[:, None] >= tile_end[None, :]).astype(jnp.int32), axis=1),
                             n_groups - 1)
    first_row = (tile_ids - tile_start[tile_group]) * tm
    tile_valid = jnp.clip(counts[tile_group] - first_row, 0, tm)
    order = jnp.argsort(group * (1 << n_exp) + active_bits, stable=True).astype(jnp.int32)
    slot_group = jnp.repeat(tile_group, tm)
    rank = jnp.repeat(first_row, tm) + jnp.tile(jnp.arange(tm, dtype=jnp.int32), n_tiles)
    rank = jnp.minimum(rank, counts[slot_group] - 1)
    src = order[jnp.clip(count_start[slot_group] + rank, 0, t - 1)]
    comb_sorted = comb[src]
    tile_active = jnp.any((comb_sorted[:, :n_exp] != 0.0).reshape(n_tiles, tm, n_exp), axis=1)
    in_group = jnp.take_along_axis(
        tile_active, tile_group[:, None] * per_group + jnp.arange(per_group)[None, :], axis=1)
    in_group = jnp.logical_and(in_group, (tile_ids < tile_end[-1])[:, None])
    first_needed = jnp.argsort(jnp.logical_not(in_group), axis=1, stable=True)
    need = (jnp.arange(per_group)[None, :] < jnp.sum(in_group.astype(jnp.int32), axis=1)[:, None]).reshape(-1)
    step_expert = (tile_group[:, None] * per_group + first_needed).reshape(-1)
    steps = jnp.arange(n_tiles * per_group, dtype=jnp.int32)
    last_needed = lax.cummax(jnp.where(need, steps, -1), axis=0)
    resident = step_expert[jnp.maximum(last_needed, 0)]
    return (src, tile_valid.astype(jnp.int32), tile_end[-1:].astype(jnp.int32),
            need.astype(jnp.int32), resident.astype(jnp.int32), comb_sorted)


def _moe_kernel(src_ref, valid_ref, used_ref, need_ref, resident_ref,
                h1_hbm, comb_ref, wg_ref, wu_ref, wd_ref, g_ref, b_ref,
                y_hbm, xbuf, xb_ref, acc_ref, obuf, sem_in, sem_out, *, alpha, tm, per_group):
    i = pl.program_id(0)
    j = pl.program_id(1)
    n_used = used_ref[0]
    slot = lax.rem(i, 2)
    live = i < n_used

    def row_in(tile, r, buf_slot):
        return pltpu.make_async_copy(h1_hbm.at[pl.ds(src_ref[tile * tm + r], 1), :],
                                     xbuf.at[buf_slot, pl.ds(r, 1), :], sem_in.at[buf_slot])

    def row_out(tile, r):
        return pltpu.make_async_copy(obuf.at[pl.ds(r, 1), :],
                                     y_hbm.at[pl.ds(src_ref[tile * tm + r], 1), :], sem_out)

    def gather(tile, buf_slot):
        def issue(r, c):
            row_in(tile, r, buf_slot).start()
            return c
        lax.fori_loop(0, tm, issue, 0, unroll=8)

    def wait_gather(buf_slot):
        pltpu.make_async_copy(h1_hbm.at[pl.ds(0, tm), :], xbuf.at[buf_slot], sem_in.at[buf_slot]).wait()

    def drain_scatter(tile):
        def wait(r, c):
            row_out(tile, 0).wait()
            return c
        lax.fori_loop(0, valid_ref[tile], wait, 0)

    def expert_step():
        xb = xb_ref[...]
        gate = _dot(xb, wg_ref[...])
        up = _dot(xb, wu_ref[...])
        hidden = (gate / (1.0 + jnp.exp(-gate))) * up
        lane = lax.broadcasted_iota(jnp.int32, comb_ref.shape, 1)
        expert = resident_ref[i * per_group + j]
        weight = jnp.sum(jnp.where(lane == expert, comb_ref[...], 0.0), axis=1, keepdims=True)
        return weight * _dot(hidden.astype(BF16), wd_ref[...])

    @pl.when(jnp.logical_and(live, j == 0))
    def _():
        @pl.when(i == 0)
        def _():
            gather(0, 0)
        wait_gather(slot)
        xb_ref[...] = xbuf[slot].astype(BF16)
        nxt = jnp.minimum(i + 1, n_used - 1)
        for r in range(tm):
            row_in(nxt, r, 1 - slot).start()
        acc_ref[...] = expert_step()

    @pl.when(jnp.logical_and(jnp.logical_and(live, j > 0), need_ref[i * per_group + j] > 0))
    def _():
        acc_ref[...] += expert_step()

    @pl.when(jnp.logical_and(live, j == per_group - 1))
    def _():
        @pl.when(i > 0)
        def _():
            drain_scatter(i - 1)
        obuf[...] = _layer_norm(alpha * xbuf[slot] + acc_ref[...], g_ref[...], b_ref[...])

        def issue(r, c):
            row_out(i, r).start()
            return c
        lax.fori_loop(0, valid_ref[i], issue, 0)

        @pl.when(i == n_used - 1)
        def _():
            drain_scatter(i)
            wait_gather(1 - slot)


def _moe(h1, comb, w_gate, w_up, w_down, ln_g, ln_b, alpha, n_groups):
    t, d = h1.shape
    n_exp, _, d_e = w_gate.shape
    per_group = n_exp // n_groups
    tm = min(512, t)
    assert t % tm == 0
    src, tile_valid, n_used, need, resident, comb_sorted = _dispatch_plan(comb, n_exp, n_groups, tm)
    n_tiles = tile_valid.shape[0]
    blocks = tm * LANES * 4 + 3 * d * d_e * 2
    scratch = 2 * tm * d * 4 + tm * d * 2 + 2 * tm * d * 4
    expert_idx = lambda i, j, src, valid, used, need, res: (res[i * per_group + j], 0, 0)
    fixed = lambda i, j, *prefetch: (0, 0)
    return pl.pallas_call(
        functools.partial(_moe_kernel, alpha=alpha, tm=tm, per_group=per_group),
        grid_spec=pltpu.PrefetchScalarGridSpec(
            num_scalar_prefetch=5,
            grid=(n_tiles, per_group),
            in_specs=[pl.BlockSpec(memory_space=pl.ANY),
                      pl.BlockSpec((tm, LANES), lambda i, j, *prefetch: (i, 0)),
                      pl.BlockSpec((None, d, d_e), expert_idx),
                      pl.BlockSpec((None, d, d_e), expert_idx),
                      pl.BlockSpec((None, d_e, d), expert_idx),
                      pl.BlockSpec((1, d), fixed), pl.BlockSpec((1, d), fixed)],
            out_specs=pl.BlockSpec(memory_space=pl.ANY),
            scratch_shapes=[pltpu.VMEM((2, tm, d), F32), pltpu.VMEM((tm, d), BF16), pltpu.VMEM((tm, d), F32),
                            pltpu.VMEM((tm, d), F32), pltpu.SemaphoreType.DMA((2,)), pltpu.SemaphoreType.DMA]),
        out_shape=jax.ShapeDtypeStruct((t, d), F32),
        compiler_params=pltpu.CompilerParams(
            dimension_semantics=("arbitrary", "arbitrary"),
            vmem_limit_bytes=_vmem_limit(blocks, scratch)),
        name="moe",
    )(src, tile_valid, n_used, need, resident, h1, comb_sorted, w_gate, w_up, w_down, ln_g, ln_b)


def kernel(x_prompt, x_sample, cache_sb_k, cache_sb_v, cache_da_k, cache_da_v, w_in, w_out, lambda_q1, lambda_k1, lambda_q2, lambda_k2, subln_g, ln1_g, ln1_b, w_coarse, b_coarse, w_fine, b_fine, w_gate, w_up, w_down, ln2_g, ln2_b):
    depth, dec_batch, past, sb_heads, sb_hd = cache_sb_k.shape
    _, _, _, da_heads, _, da_hd = cache_da_k.shape
    da_vd = cache_da_v.shape[-1]
    batch, seq, d_model = x_prompt.shape
    dec_seq = x_sample.shape[1]
    n_groups, per_group = w_fine.shape[2], w_fine.shape[3]
    n_exp = n_groups * per_group
    assert n_exp + n_groups <= LANES
    sb_w = sb_heads * sb_hd
    assert sb_w == da_heads * 2 * da_hd == da_heads * da_vd == SUBLANES * LANES, \
        "projection column groups are addressed as equal-width blocks of one token tile"
    alpha = (2 * depth) ** 0.25
    slopes = tuple(2.0 ** (-8.0 * (h + 1) / da_heads) for h in range(da_heads))
    slopes_arr = jnp.asarray(slopes, F32)

    hp = x_prompt.reshape(batch * seq, d_model)
    hs = x_sample.reshape(dec_batch * dec_seq, d_model)
    rows_p, rows_s = [], []
    for l in range(depth):
        lam_init = 0.8 - 0.6 * math.exp(-0.3 * l)
        w_in_b = w_in[l].astype(BF16)
        w_out_b = w_out[l].astype(BF16)
        lam_vecs = jnp.stack([lambda_q1[l], lambda_k1[l], lambda_q2[l], lambda_k2[l]]).astype(F32)
        gain = subln_g[l].reshape(1, da_vd)
        w_route = jnp.concatenate(
            [w_fine[l].reshape(d_model, n_exp), w_coarse[l],
             jnp.zeros((d_model, LANES - n_exp - n_groups), F32)], axis=1)
        w_route_b = jnp.concatenate(_split_bf16(w_route), axis=1)
        b_route = jnp.concatenate(
            [b_fine[l].reshape(n_exp), b_coarse[l], jnp.zeros((LANES - n_exp - n_groups,), F32)]).reshape(1, LANES)
        wg_b, wu_b, wd_b = w_gate[l].astype(BF16), w_up[l].astype(BF16), w_down[l].astype(BF16)
        g1, b1 = ln1_g[l].reshape(1, d_model), ln1_b[l].reshape(1, d_model)
        g2, b2 = ln2_g[l].reshape(1, d_model), ln2_b[l].reshape(1, d_model)

        halves = da_vd // LANES
        dv_slots = tuple(h * halves + half for half in range(halves) for h in range(da_heads))
        plain_slots = tuple(range(SUBLANES))

        sb_scale = sb_hd ** -0.5 * LOG2E
        da_scale = da_hd ** -0.5 * LOG2E

        def project(x, tag):
            sq, sk_b, sv_b, sk, sv = _proj(
                x, w_in_b, 0, ((sb_scale, None), (None, plain_slots), (None, plain_slots)), "proj_sb_" + tag)
            dq, dk_b, dv_b, dk, dv = _proj(
                x, w_in_b, 3, ((da_scale, None), (None, plain_slots), (None, dv_slots)), "proj_da_" + tag)
            return (sq, sk_b, sv_b, dq, dk_b, dv_b), (sk, sv, dk, dv)

        def dv_tiles_to_rows(tiles, lead):
            return tiles.reshape(lead + (halves, da_heads, LANES)).swapaxes(-3, -2).reshape(lead + (da_heads, da_vd))

        def dv_rows_to_tiles(rows_, n):
            return rows_.reshape(n, da_heads, halves, LANES).swapaxes(1, 2).reshape(n * SUBLANES, LANES)

        (sq, sk, sv, dq, dk, dv), new_p = project(hp, "p")
        sb_o = _sb_prompt(sq, sk, sv, batch, seq, sb_heads, sb_hd)
        da_o = _da_prompt(dq, dk, dv, slopes_arr, lam_vecs, gain, batch, seq, da_heads, da_hd, da_vd, lam_init)
        h1, comb = _post_attn(sb_o, da_o, hp, w_out_b, g1, b1, w_route_b, b_route, alpha, n_groups, per_group)
        hp = _moe(h1, comb, wg_b, wu_b, wd_b, g2, b2, alpha, n_groups)

        (sq, sk, sv, dq, dk, dv), new_s = project(hs, "s")
        rows = dec_batch * past
        sb_o = _sb_sample(sq, sk, sv, cache_sb_k[l].reshape(rows * SUBLANES, LANES),
                          cache_sb_v[l].reshape(rows * SUBLANES, LANES),
                          dec_batch, dec_seq, past, sb_heads, sb_hd)
        da_o = _da_sample(dq, dk, dv, cache_da_k[l].reshape(rows * SUBLANES, LANES),
                          dv_rows_to_tiles(cache_da_v[l], rows),
                          slopes, lam_vecs, gain, dec_batch, dec_seq, past, da_heads, da_hd, da_vd, lam_init)
        h1, comb = _post_attn(sb_o, da_o, hs, w_out_b, g1, b1, w_route_b, b_route, alpha, n_groups, per_group)
        hs = _moe(h1, comb, wg_b, wu_b, wd_b, g2, b2, alpha, n_groups)
        for new, lead, rows_out in ((new_p, (batch, seq), rows_p), (new_s, (dec_batch, dec_seq), rows_s)):
            rows_out.append((new[0].reshape(lead + (sb_heads, sb_hd)), new[1].reshape(lead + (sb_heads, sb_hd)),
                             new[2].reshape(lead + (da_heads, 2, da_hd)), dv_tiles_to_rows(new[3], lead)))

    def stack(rows, i):
        return rows[0][i][None] if len(rows) == 1 else jnp.stack([r[i] for r in rows], axis=0)

    return (hp.reshape(batch, seq, d_model), hs.reshape(dec_batch, dec_seq, d_model),
            *(stack(rows_p, i) for i in range(4)), *(stack(rows_s, i) for i in range(4)))
```

```python
import functools
import math

import jax
import jax.numpy as jnp
from jax import lax
from jax.experimental import pallas as pl
from jax.experimental.pallas import tpu as pltpu

F32 = jnp.float32
BF16 = jnp.bfloat16

CHUNK = 64
LN_EPS = 1e-5
RMS_EPS = 1e-5
NEG_INF = -1e30
LOG2E = 1.4426950408889634
LANES = 128
SUBLANES = 8
VMEM_CAP = 64 * 2**20
SB_SKIP_LOG2 = 150.0
SCATTER_UNROLL = 8
SCATTER_WAIT_ROWS = 64


def _vmem_limit(block_bytes, scratch_bytes=0):
    est = 2 * block_bytes + scratch_bytes + 16 * 2**20
    return int(min(est, VMEM_CAP - 6 * 2**20))


def _dot(a, b):
    return jnp.dot(a, b, preferred_element_type=F32)


def _dot_nt(a, b):
    return lax.dot_general(a, b, (((1,), (1,)), ((), ())), preferred_element_type=F32)


def _widen(x, width):
    if width < LANES:
        return x[:, :width]
    return x if width == LANES else jnp.concatenate([x] * (width // LANES), axis=1)


def _split_bf16(x):
    hi = x.astype(BF16)
    return hi, (x - hi.astype(F32)).astype(BF16)


def _layer_norm(y, g, b):
    mu = jnp.mean(y, axis=1, keepdims=True)
    yc = y - mu
    var = jnp.mean(yc * yc, axis=1, keepdims=True)
    return yc * lax.rsqrt(var + LN_EPS) * g + b


def _tile_rows(ref, slot, n):
    return ref[pl.ds(slot, n, stride=SUBLANES), :]


def _proj_kernel(x_ref, w_ref, *out_refs, groups):
    xb = x_ref[...].astype(BF16)
    width = SUBLANES * LANES
    o32_refs = iter(out_refs[len(groups):])
    for g, (scale, slots) in enumerate(groups):
        acc = _dot(xb, w_ref[:, g * width:(g + 1) * width])
        out_refs[g][...] = (acc if scale is None else acc * scale).astype(BF16)
        if slots:
            o32_ref = next(o32_refs)
            for j, cb in enumerate(slots):
                o32_ref[pl.ds(j, acc.shape[0], stride=SUBLANES), :] = acc[:, cb * LANES:(cb + 1) * LANES]


def _proj(x, w, first_group, groups, name):
    t, d = x.shape
    width = SUBLANES * LANES
    n = len(groups)
    assert first_group % n == 0
    tm = min(512, t)
    assert t % tm == 0
    n32 = sum(1 for _, slots in groups if slots)
    blk = tm * d * 4 + d * n * width * 2 + tm * width * (2 * n + 4 * n32)
    out_specs = ([pl.BlockSpec((tm, width), lambda i: (i, 0))] * n
                 + [pl.BlockSpec((tm * SUBLANES, LANES), lambda i: (i, 0))] * n32)
    out_shape = ([jax.ShapeDtypeStruct((t, width), BF16)] * n
                 + [jax.ShapeDtypeStruct((t * SUBLANES, LANES), F32)] * n32)
    return pl.pallas_call(
        functools.partial(_proj_kernel, groups=groups),
        grid=(t // tm,),
        in_specs=[pl.BlockSpec((tm, d), lambda i: (i, 0)),
                  pl.BlockSpec((d, n * width), lambda i: (0, first_group // n))],
        out_specs=out_specs,
        out_shape=out_shape,
        compiler_params=pltpu.CompilerParams(
            dimension_semantics=("arbitrary",), vmem_limit_bytes=_vmem_limit(blk)),
        name=name,
    )(x, w)


def _inclusive_suffix_matrix(n):
    r = lax.broadcasted_iota(jnp.int32, (2 * n, n), 0)
    c = lax.broadcasted_iota(jnp.int32, (2 * n, n), 1)
    return (jnp.where(r >= n, r - n, r) >= c).astype(BF16)


def _strict_causal_mask(nq, nk):
    r = lax.broadcasted_iota(jnp.int32, (nq, nk), 0)
    c = lax.broadcasted_iota(jnp.int32, (nq, nk), 1)
    return c < r


def _sb_blocks(qs, ks, vs, carries, accs, umat, mask):
    n = len(qs)
    nq = qs[0].shape[0]
    zs = [_dot_nt(qs[i], ks[i]) for i in range(n)]
    splits = []
    for z in zs:
        neg_abs = lax.bitcast_convert_type(
            lax.bitcast_convert_type(z, jnp.uint32) | jnp.uint32(0x80000000), F32)
        sp = jnp.maximum(z, 0.0) + jnp.log(1.0 + jnp.exp2(neg_abs)) * LOG2E
        if mask is not None:
            sp = jnp.where(mask, sp, 0.0)
        splits.append(jnp.concatenate(_split_bf16(sp), axis=1))
    incl_all = _dot(jnp.concatenate(splits, axis=0) if n > 1 else splits[0], umat)
    out_c, out_a = [], []
    for i in range(n):
        incl = incl_all[i * nq:(i + 1) * nq]
        w = jnp.exp2(zs[i] - incl - carries[i])
        if mask is not None:
            w = jnp.where(mask, w, 0.0)
        out_a.append(accs[i] + _dot(w.astype(BF16), vs[i]))
        out_c.append(carries[i] + incl[:, :1])
    return out_c, out_a


def _all_exhausted(carries):
    lowest = functools.reduce(jnp.minimum, carries)
    return jnp.min(lowest) >= SB_SKIP_LOG2


def _sb_prompt_kernel(q_ref, k_ref, v_ref, o_ref, carry_ref, acc_ref, *, blk, hd):
    qi = pl.program_id(2)
    nh = q_ref.shape[1] // hd
    umat = _inclusive_suffix_matrix(blk)
    sls = [slice(h * hd, (h + 1) * hd) for h in range(nh)]

    def run(start, carries, accs, mask):
        cs, as_ = _sb_blocks([q_ref[:, sl] for sl in sls], [k_ref[pl.ds(start, blk), sl] for sl in sls],
                             [v_ref[pl.ds(start, blk), sl] for sl in sls], carries, accs, umat, mask)
        for h in range(nh):
            carry_ref[h] = cs[h]
            acc_ref[:, sls[h]] = as_[h]
        return cs

    run(pl.multiple_of(qi * blk, blk), [jnp.zeros((blk, 1), F32)] * nh, [jnp.zeros((blk, hd), F32)] * nh,
        _strict_causal_mask(blk, blk))

    def cond(state):
        j, go = state
        return jnp.logical_and(j >= 0, go > 0)

    def body(state):
        j, _ = state
        cs = run(pl.multiple_of(j * blk, blk), [carry_ref[h] for h in range(nh)],
                 [acc_ref[:, sl] for sl in sls], None)
        return j - 1, jnp.logical_not(_all_exhausted(cs)).astype(jnp.int32)

    lax.while_loop(cond, body, (qi - 1, jnp.int32(1)))
    o_ref[...] = acc_ref[...].astype(o_ref.dtype)


def _sb_prompt(q, k, v, batch, seq, heads, hd):
    blk = min(256, seq)
    hb = math.gcd(heads, 4)
    assert seq % blk == 0
    nq = seq // blk
    w = hb * hd
    blocks = blk * w * 2 * 2 + 2 * seq * w * 2
    scratch = hb * blk * LANES * 4 + blk * w * 4
    return pl.pallas_call(
        functools.partial(_sb_prompt_kernel, blk=blk, hd=hd),
        grid=(batch, heads // hb, nq),
        in_specs=[pl.BlockSpec((blk, w), lambda b, h, i: (b * nq + i, h)),
                  pl.BlockSpec((seq, w), lambda b, h, i: (b, h)),
                  pl.BlockSpec((seq, w), lambda b, h, i: (b, h))],
        out_specs=pl.BlockSpec((blk, w), lambda b, h, i: (b * nq + i, h)),
        out_shape=jax.ShapeDtypeStruct(q.shape, BF16),
        scratch_shapes=[pltpu.VMEM((hb, blk, 1), F32), pltpu.VMEM((blk, w), F32)],
        compiler_params=pltpu.CompilerParams(
            dimension_semantics=("arbitrary", "arbitrary", "arbitrary"),
            vmem_limit_bytes=_vmem_limit(blocks, scratch)),
        name="sb_prompt",
    )(q, k, v)


def _sb_sample_kernel(q_ref, kn_ref, vn_ref, kc_hbm, vc_hbm, o_ref, kbuf, vbuf, carry_ref, acc_ref, u_ref, sem,
                      *, heads, hd, bk, nkb):
    b = pl.program_id(0)
    nq = q_ref.shape[0]
    rows = bk * SUBLANES
    slot = lax.rem(b, 2)
    sls = [slice(h * hd, (h + 1) * hd) for h in range(heads)]

    def fetch(stream, block, buf):
        start = pl.multiple_of((stream * nkb + block) * rows, rows)
        return (pltpu.make_async_copy(kc_hbm.at[pl.ds(start, rows), :], kbuf.at[buf], sem.at[0, buf]),
                pltpu.make_async_copy(vc_hbm.at[pl.ds(start, rows), :], vbuf.at[buf], sem.at[1, buf]))

    def run(ks, vs, carries, accs, umat, mask):
        cs, as_ = _sb_blocks([q_ref[:, sl] for sl in sls], ks, vs, carries, accs, umat, mask)
        for h in range(heads):
            carry_ref[h] = cs[h]
            acc_ref[:, sls[h]] = as_[h]
        return jnp.logical_not(_all_exhausted(cs)).astype(jnp.int32)

    def run_cached(buf):
        return run([kbuf[buf, pl.ds(h, bk, stride=SUBLANES), :].astype(BF16) for h in range(heads)],
                   [vbuf[buf, pl.ds(h, bk, stride=SUBLANES), :].astype(BF16) for h in range(heads)],
                   [carry_ref[h] for h in range(heads)], [acc_ref[:, sl] for sl in sls], u_ref[...], None)

    @pl.when(b == 0)
    def _():
        u_ref[...] = _inclusive_suffix_matrix(bk)
        for copy in fetch(0, nkb - 1, 0):
            copy.start()

    run([kn_ref[:, sl] for sl in sls], [vn_ref[:, sl] for sl in sls],
        [jnp.zeros((nq, 1), F32)] * heads, [jnp.zeros((nq, hd), F32)] * heads,
        _inclusive_suffix_matrix(nq), _strict_causal_mask(nq, nq))

    for copy in fetch(b, nkb - 1, slot):
        copy.wait()

    @pl.when(b + 1 < pl.num_programs(0))
    def _():
        for copy in fetch(b + 1, nkb - 1, 1 - slot):
            copy.start()

    go = run_cached(slot)

    def cond(state):
        j, go = state
        return jnp.logical_and(j >= 0, go > 0)

    def body(state):
        j, _ = state
        copies = fetch(b, j, 2)
        for copy in copies:
            copy.start()
        for copy in copies:
            copy.wait()
        return j - 1, run_cached(2)

    lax.while_loop(cond, body, (jnp.int32(nkb - 2), go))
    o_ref[...] = acc_ref[...].astype(o_ref.dtype)


def _sb_sample(q, kn, vn, kc, vc, batch, nq, past, heads, hd):
    assert heads == SUBLANES and hd == LANES
    width = heads * hd
    bk = min(512, past)
    assert past % bk == 0
    nkb = past // bk
    new_spec = pl.BlockSpec((nq, width), lambda b: (b, 0))
    blocks = 4 * nq * width * 2
    scratch = 2 * 3 * bk * width * 4 + heads * nq * LANES * 4 + nq * width * 4 + 2 * bk * bk * 2
    return pl.pallas_call(
        functools.partial(_sb_sample_kernel, heads=heads, hd=hd, bk=bk, nkb=nkb),
        grid=(batch,),
        in_specs=[new_spec, new_spec, new_spec, pl.BlockSpec(memory_space=pl.ANY), pl.BlockSpec(memory_space=pl.ANY)],
        out_specs=new_spec,
        out_shape=jax.ShapeDtypeStruct(q.shape, BF16),
        scratch_shapes=[pltpu.VMEM((3, bk * SUBLANES, LANES), F32), pltpu.VMEM((3, bk * SUBLANES, LANES), F32),
                        pltpu.VMEM((heads, nq, 1), F32), pltpu.VMEM((nq, width), F32),
                        pltpu.VMEM((2 * bk, bk), BF16), pltpu.SemaphoreType.DMA((2, 3))],
        compiler_params=pltpu.CompilerParams(
            dimension_semantics=("arbitrary",), vmem_limit_bytes=_vmem_limit(blocks, scratch)),
        name="sb_sample",
    )(q, kn, vn, kc, vc)


def _da_steps(ss, vs, m_refs, l_refs, acc_refs):
    n = len(ss)
    nk = ss[0].shape[1]
    m_olds = [r[...] for r in m_refs]
    m_news = [jnp.maximum(m_olds[i], jnp.max(ss[i], axis=1, keepdims=True)) for i in range(n)]
    ps = [jnp.exp2(ss[i] - _widen(m_news[i], nk)) for i in range(n)]
    for i in range(n):
        alpha = jnp.exp2(m_olds[i] - m_news[i])
        l_refs[i][...] = alpha * l_refs[i][...] + jnp.sum(ps[i], axis=1, keepdims=True)
        acc_refs[i][...] = (_widen(alpha, vs[i].shape[1]) * acc_refs[i][...]
                            + _dot(ps[i].astype(BF16), vs[i]))
        m_refs[i][...] = m_news[i]


def _da_same_block_bias(nq, nk, slope2):
    r = lax.broadcasted_iota(jnp.int32, (nq, nk), 0)
    c = lax.broadcasted_iota(jnp.int32, (nq, nk), 1)
    return slope2 * (r - jnp.abs(r - c)).astype(F32)


def _da_chunk_mask(nq, nk):
    r = lax.broadcasted_iota(jnp.int32, (nq, nk), 0)
    c = lax.broadcasted_iota(jnp.int32, (nq, nk), 1)
    return (c // CHUNK) <= (r // CHUNK)


def _da_lambda(lam_ref, lam_init):
    lv = lam_ref[...]
    a = jnp.sum(lv[0:1] * lv[1:2], axis=1, keepdims=True)
    b = jnp.sum(lv[2:3] * lv[3:4], axis=1, keepdims=True)
    return jnp.exp(a) - jnp.exp(b) + lam_init


def _da_finish(acc0, l0, acc1, l1, lam, gain, lam_init):
    vd = acc0.shape[1]
    o = acc0 * _widen(1.0 / l0, vd) - lam * (acc1 * _widen(1.0 / l1, vd))
    y = o * lax.rsqrt(jnp.mean(o * o, axis=1, keepdims=True) + RMS_EPS)
    return (y * gain) * (1.0 - lam_init)


def _da_prompt_kernel(slope_ref, q_ref, k_ref, v_ref, lam_ref, g_ref, o_ref, m_ref, l_ref, acc_ref,
                      *, blk, hd, vd, hb, lam_init):
    hg = pl.program_id(1)
    qi = pl.program_id(2)
    m_ref[...] = jnp.full(m_ref.shape, NEG_INF, F32)
    l_ref[...] = jnp.zeros(l_ref.shape, F32)
    acc_ref[...] = jnp.zeros(acc_ref.shape, F32)
    col = lax.broadcasted_iota(jnp.int32, (1, blk), 1)
    chains = lambda ref: [ref.at[i] for i in range(2 * hb)]
    slopes2 = [slope_ref[hg * hb + h] * LOG2E for h in range(hb)]

    def step(j, bias_of, mask):
        st = pl.multiple_of(j * blk, blk)
        ss, vs = [], []
        for h in range(hb):
            bias = bias_of(slopes2[h])
            v = v_ref[pl.ds(st, blk), h * vd:(h + 1) * vd]
            for m in range(2):
                sl = slice((2 * h + m) * hd, (2 * h + m + 1) * hd)
                s = _dot_nt(q_ref[:, sl], k_ref[pl.ds(st, blk), sl]) + bias
                ss.append(s if mask is None else jnp.where(mask, s, NEG_INF))
                vs.append(v)
        _da_steps(ss, vs, chains(m_ref), chains(l_ref), chains(acc_ref))

    def body(j, c):
        rel = (col + (j - qi) * blk).astype(F32)
        step(j, lambda slope2: slope2 * rel, None)
        return c

    lax.fori_loop(0, qi, body, 0)
    step(qi, lambda slope2: _da_same_block_bias(blk, blk, slope2), _da_chunk_mask(blk, blk))
    lam = _da_lambda(lam_ref, lam_init)
    for h in range(hb):
        o_ref[:, h * vd:(h + 1) * vd] = _da_finish(
            acc_ref[2 * h], l_ref[2 * h], acc_ref[2 * h + 1], l_ref[2 * h + 1],
            lam, g_ref[...], lam_init).astype(o_ref.dtype)


def _da_prompt(q, k, v, slopes, lam_vecs, gain, batch, seq, heads, hd, vd, lam_init):
    blk = min(512, seq)
    hb = math.gcd(heads, 2)
    assert seq % blk == 0 and blk % CHUNK == 0
    nq = seq // blk
    blocks = hb * (blk * 2 * hd * 2 + blk * vd * 2 + seq * 2 * hd * 2 + seq * vd * 2)
    scratch = hb * (2 * 2 * blk * LANES * 4 + 2 * blk * vd * 4)
    return pl.pallas_call(
        functools.partial(_da_prompt_kernel, blk=blk, hd=hd, vd=vd, hb=hb, lam_init=lam_init),
        grid=(batch, heads // hb, nq),
        in_specs=[pl.BlockSpec(memory_space=pltpu.SMEM),
                  pl.BlockSpec((blk, hb * 2 * hd), lambda b, h, i: (b * nq + i, h)),
                  pl.BlockSpec((seq, hb * 2 * hd), lambda b, h, i: (b, h)),
                  pl.BlockSpec((seq, hb * vd), lambda b, h, i: (b, h)),
                  pl.BlockSpec((4, hd), lambda b, h, i: (0, 0)),
                  pl.BlockSpec((1, vd), lambda b, h, i: (0, 0))],
        out_specs=pl.BlockSpec((blk, hb * vd), lambda b, h, i: (b * nq + i, h)),
        out_shape=jax.ShapeDtypeStruct(v.shape, BF16),
        scratch_shapes=[pltpu.VMEM((2 * hb, blk, LANES), F32), pltpu.VMEM((2 * hb, blk, LANES), F32),
                        pltpu.VMEM((2 * hb, blk, vd), F32)],
        compiler_params=pltpu.CompilerParams(
            dimension_semantics=("arbitrary", "arbitrary", "arbitrary"),
            vmem_limit_bytes=_vmem_limit(blocks, scratch)),
        name="da_prompt",
    )(slopes, q, k, v, lam_vecs, gain)


def _da_sample_kernel(q_ref, kn_ref, vn_ref, kc_ref, vc_ref, lam_ref, g_ref, o_ref,
                      m_ref, l_ref, acc_ref, *, heads, hd, vd, past, slopes, lam_init):
    kb = pl.program_id(1)
    nkb = pl.num_programs(1) - 1
    nq = q_ref.shape[0]
    bk = kc_ref.shape[0] // SUBLANES
    chains = 2 * heads
    refs = lambda ref: [ref.at[i] for i in range(chains)]

    @pl.when(kb == 0)
    def _():
        m_ref[...] = jnp.full(m_ref.shape, NEG_INF, F32)
        l_ref[...] = jnp.zeros(l_ref.shape, F32)
        acc_ref[...] = jnp.zeros(acc_ref.shape, F32)

    def step(k_of, v_of, bias_of):
        ss, vs = [], []
        for h in range(heads):
            v = v_of(h)
            bias = bias_of(slopes[h] * LOG2E)
            for m in range(2):
                i = 2 * h + m
                ss.append(_dot_nt(q_ref[:, i * hd:(i + 1) * hd], k_of(i)) + bias)
                vs.append(v)
        _da_steps(ss, vs, refs(m_ref), refs(l_ref), refs(acc_ref))

    @pl.when(kb < nkb)
    def _():
        col = lax.broadcasted_iota(jnp.int32, (1, bk), 1)
        rel = (col + (kb * bk - past)).astype(F32)
        step(lambda i: _tile_rows(kc_ref, i, bk).astype(BF16),
             lambda h: jnp.concatenate([_tile_rows(vc_ref, half * heads + h, bk) for half in range(vd // LANES)],
                                       axis=1).astype(BF16),
             lambda slope2: slope2 * rel)

    @pl.when(kb == nkb)
    def _():
        step(lambda i: kn_ref[:, i * hd:(i + 1) * hd], lambda h: vn_ref[:, h * vd:(h + 1) * vd],
             lambda slope2: _da_same_block_bias(nq, nq, slope2))
        lam = _da_lambda(lam_ref, lam_init)
        for h in range(heads):
            o_ref[:, h * vd:(h + 1) * vd] = _da_finish(
                acc_ref[2 * h], l_ref[2 * h], acc_ref[2 * h + 1], l_ref[2 * h + 1],
                lam, g_ref[...], lam_init).astype(o_ref.dtype)


def _da_sample(q, kn, vn, kc, vc, slopes, lam_vecs, gain, batch, nq, past, heads, hd, vd, lam_init):
    assert nq == CHUNK and past % CHUNK == 0
    assert hd == LANES and 2 * heads == SUBLANES and heads * (vd // LANES) == SUBLANES
    qk_w = heads * 2 * hd
    v_w = heads * vd
    bk = min(512, past)
    assert past % bk == 0
    nkb = past // bk
    blocks = 2 * nq * qk_w * 2 + 2 * nq * v_w * 2 + bk * (qk_w + v_w) * 4
    scratch = 2 * 2 * heads * nq * LANES * 4 + 2 * heads * nq * vd * 4
    cache_idx = lambda b, j: (b * nkb + jnp.minimum(j, nkb - 1), 0)
    return pl.pallas_call(
        functools.partial(_da_sample_kernel, heads=heads, hd=hd, vd=vd, past=past, slopes=slopes,
                          lam_init=lam_init),
        grid=(batch, nkb + 1),
        in_specs=[pl.BlockSpec((nq, qk_w), lambda b, j: (b, 0)),
                  pl.BlockSpec((nq, qk_w), lambda b, j: (b, 0)),
                  pl.BlockSpec((nq, v_w), lambda b, j: (b, 0)),
                  pl.BlockSpec((bk * SUBLANES, LANES), cache_idx),
                  pl.BlockSpec((bk * SUBLANES, LANES), cache_idx),
                  pl.BlockSpec((4, hd), lambda b, j: (0, 0)),
                  pl.BlockSpec((1, vd), lambda b, j: (0, 0))],
        out_specs=pl.BlockSpec((nq, v_w), lambda b, j: (b, 0)),
        out_shape=jax.ShapeDtypeStruct(vn.shape, BF16),
        scratch_shapes=[pltpu.VMEM((2 * heads, nq, LANES), F32), pltpu.VMEM((2 * heads, nq, LANES), F32),
                        pltpu.VMEM((2 * heads, nq, vd), F32)],
        compiler_params=pltpu.CompilerParams(
            dimension_semantics=("arbitrary", "arbitrary"),
            vmem_limit_bytes=_vmem_limit(blocks, scratch)),
        name="da_sample",
    )(q, kn, vn, kc, vc, lam_vecs, gain)


def _route(logits, n_groups, per_group):
    n_exp = n_groups * per_group
    lane = lax.broadcasted_iota(jnp.int32, logits.shape, 1).astype(F32)
    big = jnp.float32(3e38)

    def first_argmax(x):
        top = jnp.max(x, axis=1, keepdims=True)
        idx = jnp.min(jnp.where(x == top, lane, big), axis=1, keepdims=True)
        return top, idx

    is_coarse = (lane >= n_exp) & (lane < n_exp + n_groups)
    coarse = jnp.where(is_coarse, logits, -big)
    c_top, c_idx = first_argmax(coarse)
    g_gate = 1.0 / jnp.sum(jnp.where(is_coarse, jnp.exp(logits - c_top), 0.0), axis=1, keepdims=True)
    g_first = (c_idx - n_exp) * per_group
    in_group = (lane >= g_first) & (lane < g_first + per_group)
    fine = jnp.where(in_group, logits, -big)
    v1, i1 = first_argmax(fine)
    v2, i2 = first_argmax(jnp.where(lane == i1, -big, fine))
    e2 = jnp.exp(v2 - v1)
    w1 = g_gate / (1.0 + e2)
    w2 = g_gate * e2 / (1.0 + e2)
    comb = jnp.where(lane == i1, w1, jnp.where(lane == i2, w2, 0.0))
    return jnp.where(lane == n_exp, c_idx - n_exp, comb)


def _post_attn_kernel(sb_ref, da_ref, x_ref, wsb_ref, wda_ref, g_ref, b_ref, wr_ref, br_ref,
                      h1_ref, comb_ref, *, alpha, n_groups, per_group):
    mixed = _dot(sb_ref[...], wsb_ref[...]) + _dot(da_ref[...], wda_ref[...])
    h1 = _layer_norm(alpha * x_ref[...] + mixed, g_ref[...], b_ref[...])
    h1_ref[...] = h1
    h_hi, h_lo = _split_bf16(h1)
    both = _dot(h_hi, wr_ref[...])
    logits = both[:, :LANES] + both[:, LANES:] + _dot(h_lo, wr_ref[:, :LANES]) + br_ref[...]
    comb_ref[...] = _route(logits, n_groups, per_group)


def _post_attn(sb_o, da_o, x, w_out, ln_g, ln_b, w_route, b_route, alpha, n_groups, per_group):
    t, d = x.shape
    w_sb = sb_o.shape[1]
    w_da = da_o.shape[1]
    assert w_sb == w_da
    tm = min(512, t)
    assert t % tm == 0
    blocks = (tm * (w_sb + w_da) * 2 + tm * d * 4 * 2 + (w_sb + w_da) * d * 2
              + d * 2 * LANES * 2 + tm * LANES * 4)
    row = lambda i: (i, 0)
    fixed = lambda i: (0, 0)
    return pl.pallas_call(
        functools.partial(_post_attn_kernel, alpha=alpha, n_groups=n_groups, per_group=per_group),
        grid=(t // tm,),
        in_specs=[pl.BlockSpec((tm, w_sb), row), pl.BlockSpec((tm, w_da), row), pl.BlockSpec((tm, d), row),
                  pl.BlockSpec((w_sb, d), fixed), pl.BlockSpec((w_da, d), lambda i: (1, 0)),
                  pl.BlockSpec((1, d), fixed), pl.BlockSpec((1, d), fixed),
                  pl.BlockSpec((d, 2 * LANES), fixed), pl.BlockSpec((1, LANES), fixed)],
        out_specs=[pl.BlockSpec((tm, d), row), pl.BlockSpec((tm, LANES), row)],
        out_shape=[jax.ShapeDtypeStruct((t, d), F32), jax.ShapeDtypeStruct((t, LANES), F32)],
        compiler_params=pltpu.CompilerParams(
            dimension_semantics=("arbitrary",), vmem_limit_bytes=_vmem_limit(blocks)),
        name="post_attn",
    )(sb_o, da_o, x, w_out, w_out, ln_g, ln_b, w_route, b_route)


def _dispatch_plan(comb, n_exp, n_groups, tm):
    t = comb.shape[0]
    per_group = n_exp // n_groups
    n_tiles = t // tm + n_groups
    group = comb[:, n_exp].astype(jnp.int32)
    active = comb[:, :n_exp] != 0.0
    active_bits = jnp.sum(active.astype(jnp.int32) << jnp.arange(n_exp, dtype=jnp.int32)[None, :], axis=1)
    counts = jnp.sum((group[:, None] == jnp.arange(n_groups)[None, :]).astype(jnp.int32), axis=0)
    tiles_g = (counts + tm - 1) // tm
    tile_end = jnp.cumsum(tiles_g)
    tile_start = tile_end - tiles_g
    count_start = jnp.cumsum(counts) - counts
    tile_ids = jnp.arange(n_tiles, dtype=jnp.int32)
    tile_group = jnp.minimum(jnp.sum((tile_ids[:, None] >= tile_end[None, :]).astype(jnp.int32), axis=1),
                             n_groups - 1)
    first_row = (tile_ids - tile_start[tile_group]) * tm
    tile_valid = jnp.clip(counts[tile_group] - first_row, 0, tm)
    order = jnp.argsort(group * (1 << n_exp) + active_bits, stable=True).astype(jnp.int32)
    slot_group = jnp.repeat(tile_group, tm)
    rank = jnp.repeat(first_row, tm) + jnp.tile(jnp.arange(tm, dtype=jnp.int32), n_tiles)
    rank = jnp.minimum(rank, counts[slot_group] - 1)
    src = order[jnp.clip(count_start[slot_group] + rank, 0, t - 1)]
    comb_sorted = comb[src]
    tile_active = jnp.any((comb_sorted[:, :n_exp] != 0.0).reshape(n_tiles, tm, n_exp), axis=1)
    in_group = jnp.take_along_axis(
        tile_active, tile_group[:, None] * per_group + jnp.arange(per_group)[None, :], axis=1)
    in_group = jnp.logical_and(in_group, (tile_ids < tile_end[-1])[:, None])
    first_needed = jnp.argsort(jnp.logical_not(in_group), axis=1, stable=True)
    need = (jnp.arange(per_group)[None, :] < jnp.sum(in_group.astype(jnp.int32), axis=1)[:, None]).reshape(-1)
    step_expert = (tile_group[:, None] * per_group + first_needed).reshape(-1)
    steps = jnp.arange(n_tiles * per_group, dtype=jnp.int32)
    last_needed = lax.cummax(jnp.where(need, steps, -1), axis=0)
    resident = step_expert[jnp.maximum(last_needed, 0)]
    return (src, tile_valid.astype(jnp.int32), tile_end[-1:].astype(jnp.int32),
            need.astype(jnp.int32), resident.astype(jnp.int32), comb_sorted)


def _moe_kernel(src_ref, valid_ref, used_ref, need_ref, resident_ref,
                h1_hbm, comb_ref, wg_ref, wu_ref, wd_ref, g_ref, b_ref,
                y_hbm, xbuf, xb_ref, acc_ref, obuf, sem_in, sem_out, *, alpha, tm, per_group):
    i = pl.program_id(0)
    j = pl.program_id(1)
    n_used = used_ref[0]
    slot = lax.rem(i, 2)
    live = i < n_used

    def row_in(tile, r, buf_slot):
        return pltpu.make_async_copy(h1_hbm.at[pl.ds(src_ref[tile * tm + r], 1), :],
                                     xbuf.at[buf_slot, pl.ds(r, 1), :], sem_in.at[buf_slot])

    def row_out(tile, r):
        return pltpu.make_async_copy(obuf.at[pl.ds(r, 1), :],
                                     y_hbm.at[pl.ds(src_ref[tile * tm + r], 1), :], sem_out)

    def gather(tile, buf_slot):
        def issue(r, c):
            row_in(tile, r, buf_slot).start()
            return c
        lax.fori_loop(0, tm, issue, 0, unroll=8)

    def wait_gather(buf_slot):
        pltpu.make_async_copy(h1_hbm.at[pl.ds(0, tm), :], xbuf.at[buf_slot], sem_in.at[buf_slot]).wait()

    def drain_scatter(tile):
        n = valid_ref[tile]

        def wait_chunk(c, carry):
            pltpu.make_async_copy(obuf.at[pl.ds(0, SCATTER_WAIT_ROWS), :],
                                  y_hbm.at[pl.ds(0, SCATTER_WAIT_ROWS), :], sem_out).wait()
            return carry

        def wait_row(r, carry):
            row_out(tile, 0).wait()
            return carry
        lax.fori_loop(0, n // SCATTER_WAIT_ROWS, wait_chunk, 0)
        lax.fori_loop(0, lax.rem(n, SCATTER_WAIT_ROWS), wait_row, 0)

    def expert_step():
        xb = xb_ref[...]
        gate = _dot(xb, wg_ref[...])
        up = _dot(xb, wu_ref[...])
        hidden = (gate / (1.0 + jnp.exp(-gate))) * up
        lane = lax.broadcasted_iota(jnp.int32, comb_ref.shape, 1)
        expert = resident_ref[i * per_group + j]
        weight = jnp.sum(jnp.where(lane == expert, comb_ref[...], 0.0), axis=1, keepdims=True)
        return weight * _dot(hidden.astype(BF16), wd_ref[...])

    @pl.when(jnp.logical_and(live, j == 0))
    def _():
        @pl.when(i == 0)
        def _():
            gather(0, 0)
        wait_gather(slot)
        xb_ref[...] = xbuf[slot].astype(BF16)
        nxt = jnp.minimum(i + 1, n_used - 1)
        for r in range(tm):
            row_in(nxt, r, 1 - slot).start()
        acc_ref[...] = expert_step()

    @pl.when(jnp.logical_and(jnp.logical_and(live, j > 0), need_ref[i * per_group + j] > 0))
    def _():
        acc_ref[...] += expert_step()

    @pl.when(jnp.logical_and(live, j == per_group - 1))
    def _():
        @pl.when(i > 0)
        def _():
            drain_scatter(i - 1)
        obuf[...] = _layer_norm(alpha * xbuf[slot] + acc_ref[...], g_ref[...], b_ref[...])

        n = valid_ref[i]

        def issue_group(c, carry):
            for r in range(SCATTER_UNROLL):
                row_out(i, c * SCATTER_UNROLL + r).start()
            return carry

        def issue_row(r, carry):
            row_out(i, (n // SCATTER_UNROLL) * SCATTER_UNROLL + r).start()
            return carry
        lax.fori_loop(0, n // SCATTER_UNROLL, issue_group, 0)
        lax.fori_loop(0, lax.rem(n, SCATTER_UNROLL), issue_row, 0)

        @pl.when(i == n_used - 1)
        def _():
            drain_scatter(i)
            wait_gather(1 - slot)


def _moe(h1, comb, w_gate, w_up, w_down, ln_g, ln_b, alpha, n_groups):
    t, d = h1.shape
    n_exp, _, d_e = w_gate.shape
    per_group = n_exp // n_groups
    tm = min(512, t)
    assert t % tm == 0
    src, tile_valid, n_used, need, resident, comb_sorted = _dispatch_plan(comb, n_exp, n_groups, tm)
    n_tiles = tile_valid.shape[0]
    blocks = tm * LANES * 4 + 3 * d * d_e * 2
    scratch = 2 * tm * d * 4 + tm * d * 2 + 2 * tm * d * 4
    expert_idx = lambda i, j, src, valid, used, need, res: (res[i * per_group + j], 0, 0)
    fixed = lambda i, j, *prefetch: (0, 0)
    return pl.pallas_call(
        functools.partial(_moe_kernel, alpha=alpha, tm=tm, per_group=per_group),
        grid_spec=pltpu.PrefetchScalarGridSpec(
            num_scalar_prefetch=5,
            grid=(n_tiles, per_group),
            in_specs=[pl.BlockSpec(memory_space=pl.ANY),
                      pl.BlockSpec((tm, LANES), lambda i, j, *prefetch: (i, 0)),
                      pl.BlockSpec((None, d, d_e), expert_idx),
                      pl.BlockSpec((None, d, d_e), expert_idx),
                      pl.BlockSpec((None, d_e, d), expert_idx),
                      pl.BlockSpec((1, d), fixed), pl.BlockSpec((1, d), fixed)],
            out_specs=pl.BlockSpec(memory_space=pl.ANY),
            scratch_shapes=[pltpu.VMEM((2, tm, d), F32), pltpu.VMEM((tm, d), BF16), pltpu.VMEM((tm, d), F32),
                            pltpu.VMEM((tm, d), F32), pltpu.SemaphoreType.DMA((2,)), pltpu.SemaphoreType.DMA]),
        out_shape=jax.ShapeDtypeStruct((t, d), F32),
        compiler_params=pltpu.CompilerParams(
            dimension_semantics=("arbitrary", "arbitrary"),
            vmem_limit_bytes=_vmem_limit(blocks, scratch)),
        name="moe",
    )(src, tile_valid, n_used, need, resident, h1, comb_sorted, w_gate, w_up, w_down, ln_g, ln_b)


def kernel(x_prompt, x_sample, cache_sb_k, cache_sb_v, cache_da_k, cache_da_v, w_in, w_out, lambda_q1, lambda_k1, lambda_q2, lambda_k2, subln_g, ln1_g, ln1_b, w_coarse, b_coarse, w_fine, b_fine, w_gate, w_up, w_down, ln2_g, ln2_b):
    depth, dec_batch, past, sb_heads, sb_hd = cache_sb_k.shape
    _, _, _, da_heads, _, da_hd = cache_da_k.shape
    da_vd = cache_da_v.shape[-1]
    batch, seq, d_model = x_prompt.shape
    dec_seq = x_sample.shape[1]
    n_groups, per_group = w_fine.shape[2], w_fine.shape[3]
    n_exp = n_groups * per_group
    assert n_exp + n_groups <= LANES
    sb_w = sb_heads * sb_hd
    assert sb_w == da_heads * 2 * da_hd == da_heads * da_vd == SUBLANES * LANES, \
        "projection column groups are addressed as equal-width blocks of one token tile"
    alpha = (2 * depth) ** 0.25
    slopes = tuple(2.0 ** (-8.0 * (h + 1) / da_heads) for h in range(da_heads))
    slopes_arr = jnp.asarray(slopes, F32)

    hp = x_prompt.reshape(batch * seq, d_model)
    hs = x_sample.reshape(dec_batch * dec_seq, d_model)
    rows_p, rows_s = [], []
    for l in range(depth):
        lam_init = 0.8 - 0.6 * math.exp(-0.3 * l)
        w_in_b = w_in[l].astype(BF16)
        w_out_b = w_out[l].astype(BF16)
        lam_vecs = jnp.stack([lambda_q1[l], lambda_k1[l], lambda_q2[l], lambda_k2[l]]).astype(F32)
        gain = subln_g[l].reshape(1, da_vd)
        w_route = jnp.concatenate(
            [w_fine[l].reshape(d_model, n_exp), w_coarse[l],
             jnp.zeros((d_model, LANES - n_exp - n_groups), F32)], axis=1)
        w_route_b = jnp.concatenate(_split_bf16(w_route), axis=1)
        b_route = jnp.concatenate(
            [b_fine[l].reshape(n_exp), b_coarse[l], jnp.zeros((LANES - n_exp - n_groups,), F32)]).reshape(1, LANES)
        wg_b, wu_b, wd_b = w_gate[l].astype(BF16), w_up[l].astype(BF16), w_down[l].astype(BF16)
        g1, b1 = ln1_g[l].reshape(1, d_model), ln1_b[l].reshape(1, d_model)
        g2, b2 = ln2_g[l].reshape(1, d_model), ln2_b[l].reshape(1, d_model)

        halves = da_vd // LANES
        dv_slots = tuple(h * halves + half for half in range(halves) for h in range(da_heads))
        plain_slots = tuple(range(SUBLANES))

        sb_scale = sb_hd ** -0.5 * LOG2E
        da_scale = da_hd ** -0.5 * LOG2E

        def project(x, tag):
            sq, sk_b, sv_b, sk, sv = _proj(
                x, w_in_b, 0, ((sb_scale, None), (None, plain_slots), (None, plain_slots)), "proj_sb_" + tag)
            dq, dk_b, dv_b, dk, dv = _proj(
                x, w_in_b, 3, ((da_scale, None), (None, plain_slots), (None, dv_slots)), "proj_da_" + tag)
            return (sq, sk_b, sv_b, dq, dk_b, dv_b), (sk, sv, dk, dv)

        def dv_tiles_to_rows(tiles, lead):
            return tiles.reshape(lead + (halves, da_heads, LANES)).swapaxes(-3, -2).reshape(lead + (da_heads, da_vd))

        def dv_rows_to_tiles(rows_, n):
            return rows_.reshape(n, da_heads, halves, LANES).swapaxes(1, 2).reshape(n * SUBLANES, LANES)

        (sq, sk, sv, dq, dk, dv), new_p = project(hp, "p")
        sb_o = _sb_prompt(sq, sk, sv, batch, seq, sb_heads, sb_hd)
        da_o = _da_prompt(dq, dk, dv, slopes_arr, lam_vecs, gain, batch, seq, da_heads, da_hd, da_vd, lam_init)
        h1, comb = _post_attn(sb_o, da_o, hp, w_out_b, g1, b1, w_route_b, b_route, alpha, n_groups, per_group)
        hp = _moe(h1, comb, wg_b, wu_b, wd_b, g2, b2, alpha, n_groups)

        (sq, sk, sv, dq, dk, dv), new_s = project(hs, "s")
        rows = dec_batch * past
        sb_o = _sb_sample(sq, sk, sv, cache_sb_k[l].reshape(rows * SUBLANES, LANES),
                          cache_sb_v[l].reshape(rows * SUBLANES, LANES),
                          dec_batch, dec_seq, past, sb_heads, sb_hd)
        da_o = _da_sample(dq, dk, dv, cache_da_k[l].reshape(rows * SUBLANES, LANES),
                          dv_rows_to_tiles(cache_da_v[l], rows),
                          slopes, lam_vecs, gain, dec_batch, dec_seq, past, da_heads, da_hd, da_vd, lam_init)
        h1, comb = _post_attn(sb_o, da_o, hs, w_out_b, g1, b1, w_route_b, b_route, alpha, n_groups, per_group)
        hs = _moe(h1, comb, wg_b, wu_b, wd_b, g2, b2, alpha, n_groups)
        for new, lead, rows_out in ((new_p, (batch, seq), rows_p), (new_s, (dec_batch, dec_seq), rows_s)):
            rows_out.append((new[0].reshape(lead + (sb_heads, sb_hd)), new[1].reshape(lead + (sb_heads, sb_hd)),
                             new[2].reshape(lead + (da_heads, 2, da_hd)), dv_tiles_to_rows(new[3], lead)))

    def stack(rows, i):
        return rows[0][i][None] if len(rows) == 1 else jnp.stack([r[i] for r in rows], axis=0)

    return (hp.reshape(batch, seq, d_model), hs.reshape(dec_batch, dec_seq, d_model),
            *(stack(rows_p, i) for i in range(4)), *(stack(rows_s, i) for i in range(4)))
```

```python
import functools
import math

import jax
import jax.numpy as jnp
from jax import lax
from jax.experimental import pallas as pl
from jax.experimental.pallas import tpu as pltpu

F32 = jnp.float32
BF16 = jnp.bfloat16

CHUNK = 64
LN_EPS = 1e-5
RMS_EPS = 1e-5
NEG_INF = -1e30
LOG2E = 1.4426950408889634
LANES = 128
SUBLANES = 8
VMEM_CAP = 64 * 2**20
SB_SKIP_LOG2 = 150.0
SCATTER_UNROLL = 8
SCATTER_WAIT_ROWS = 64
TOKEN_TILE = 512
SB_BLOCK = 256
SB_HEADS_PER_STEP = 4
DA_BLOCK = 512
DA_HEADS_PER_STEP = 2
SB_CACHE_BLOCK = 512
DA_CACHE_BLOCK = 1024


def _vmem_limit(block_bytes, scratch_bytes=0):
    est = 2 * block_bytes + scratch_bytes + 16 * 2**20
    return int(min(est, VMEM_CAP - 6 * 2**20))


def _dot(a, b):
    return jnp.dot(a, b, preferred_element_type=F32)


def _dot_nt(a, b):
    return lax.dot_general(a, b, (((1,), (1,)), ((), ())), preferred_element_type=F32)


def _widen(x, width):
    if width < LANES:
        return x[:, :width]
    return x if width == LANES else jnp.concatenate([x] * (width // LANES), axis=1)


def _split_bf16(x):
    hi = x.astype(BF16)
    return hi, (x - hi.astype(F32)).astype(BF16)


def _layer_norm(y, g, b):
    mu = jnp.mean(y, axis=1, keepdims=True)
    yc = y - mu
    var = jnp.mean(yc * yc, axis=1, keepdims=True)
    return yc * lax.rsqrt(var + LN_EPS) * g + b


def _tile_rows(ref, slot, n):
    return ref[pl.ds(slot, n, stride=SUBLANES), :]


def _proj_kernel(x_ref, w_ref, *out_refs, groups):
    xb = x_ref[...].astype(BF16)
    width = SUBLANES * LANES
    o32_refs = iter(out_refs[len(groups):])
    for g, (scale, slots) in enumerate(groups):
        acc = _dot(xb, w_ref[:, g * width:(g + 1) * width])
        out_refs[g][...] = (acc if scale is None else acc * scale).astype(BF16)
        if slots:
            o32_ref = next(o32_refs)
            for j, cb in enumerate(slots):
                o32_ref[pl.ds(j, acc.shape[0], stride=SUBLANES), :] = acc[:, cb * LANES:(cb + 1) * LANES]


def _proj(x, w, first_group, groups, name):
    t, d = x.shape
    width = SUBLANES * LANES
    n = len(groups)
    assert first_group % n == 0
    tm = min(TOKEN_TILE, t)
    assert t % tm == 0
    n32 = sum(1 for _, slots in groups if slots)
    blk = tm * d * 4 + d * n * width * 2 + tm * width * (2 * n + 4 * n32)
    out_specs = ([pl.BlockSpec((tm, width), lambda i: (i, 0))] * n
                 + [pl.BlockSpec((tm * SUBLANES, LANES), lambda i: (i, 0))] * n32)
    out_shape = ([jax.ShapeDtypeStruct((t, width), BF16)] * n
                 + [jax.ShapeDtypeStruct((t * SUBLANES, LANES), F32)] * n32)
    return pl.pallas_call(
        functools.partial(_proj_kernel, groups=groups),
        grid=(t // tm,),
        in_specs=[pl.BlockSpec((tm, d), lambda i: (i, 0)),
                  pl.BlockSpec((d, n * width), lambda i: (0, first_group // n))],
        out_specs=out_specs,
        out_shape=out_shape,
        compiler_params=pltpu.CompilerParams(
            dimension_semantics=("arbitrary",), vmem_limit_bytes=_vmem_limit(blk)),
        name=name,
    )(x, w)


def _inclusive_suffix_matrix(n):
    r = lax.broadcasted_iota(jnp.int32, (2 * n, n), 0)
    c = lax.broadcasted_iota(jnp.int32, (2 * n, n), 1)
    return (jnp.where(r >= n, r - n, r) >= c).astype(BF16)


def _strict_causal_mask(nq, nk):
    r = lax.broadcasted_iota(jnp.int32, (nq, nk), 0)
    c = lax.broadcasted_iota(jnp.int32, (nq, nk), 1)
    return c < r


def _sb_blocks(qs, ks, vs, carries, accs, umat, mask):
    n = len(qs)
    nq = qs[0].shape[0]
    zs = [_dot_nt(qs[i], ks[i]) for i in range(n)]
    splits = []
    for z in zs:
        neg_abs = lax.bitcast_convert_type(
            lax.bitcast_convert_type(z, jnp.uint32) | jnp.uint32(0x80000000), F32)
        sp = jnp.maximum(z, 0.0) + jnp.log(1.0 + jnp.exp2(neg_abs)) * LOG2E
        if mask is not None:
            sp = jnp.where(mask, sp, 0.0)
        splits.append(jnp.concatenate(_split_bf16(sp), axis=1))
    incl_all = _dot(jnp.concatenate(splits, axis=0) if n > 1 else splits[0], umat)
    out_c, out_a = [], []
    for i in range(n):
        incl = incl_all[i * nq:(i + 1) * nq]
        w = jnp.exp2(zs[i] - incl - carries[i])
        if mask is not None:
            w = jnp.where(mask, w, 0.0)
        out_a.append(accs[i] + _dot(w.astype(BF16), vs[i]))
        out_c.append(carries[i] + incl[:, :1])
    return out_c, out_a


def _all_exhausted(carries):
    lowest = functools.reduce(jnp.minimum, carries)
    return jnp.min(lowest) >= SB_SKIP_LOG2


def _sb_prompt_kernel(q_ref, k_ref, v_ref, o_ref, carry_ref, acc_ref, *, blk, hd):
    qi = pl.program_id(2)
    nh = q_ref.shape[1] // hd
    umat = _inclusive_suffix_matrix(blk)
    sls = [slice(h * hd, (h + 1) * hd) for h in range(nh)]

    def run(start, carries, accs, mask):
        cs, as_ = _sb_blocks([q_ref[:, sl] for sl in sls], [k_ref[pl.ds(start, blk), sl] for sl in sls],
                             [v_ref[pl.ds(start, blk), sl] for sl in sls], carries, accs, umat, mask)
        for h in range(nh):
            carry_ref[h] = cs[h]
            acc_ref[:, sls[h]] = as_[h]
        return cs

    run(pl.multiple_of(qi * blk, blk), [jnp.zeros((blk, 1), F32)] * nh, [jnp.zeros((blk, hd), F32)] * nh,
        _strict_causal_mask(blk, blk))

    def cond(state):
        j, go = state
        return jnp.logical_and(j >= 0, go > 0)

    def body(state):
        j, _ = state
        cs = run(pl.multiple_of(j * blk, blk), [carry_ref[h] for h in range(nh)],
                 [acc_ref[:, sl] for sl in sls], None)
        return j - 1, jnp.logical_not(_all_exhausted(cs)).astype(jnp.int32)

    lax.while_loop(cond, body, (qi - 1, jnp.int32(1)))
    o_ref[...] = acc_ref[...].astype(o_ref.dtype)


def _sb_prompt(q, k, v, batch, seq, heads, hd):
    blk = min(SB_BLOCK, seq)
    hb = math.gcd(heads, SB_HEADS_PER_STEP)
    assert seq % blk == 0
    nq = seq // blk
    w = hb * hd
    blocks = blk * w * 2 * 2 + 2 * seq * w * 2
    scratch = hb * blk * LANES * 4 + blk * w * 4
    return pl.pallas_call(
        functools.partial(_sb_prompt_kernel, blk=blk, hd=hd),
        grid=(batch, heads // hb, nq),
        in_specs=[pl.BlockSpec((blk, w), lambda b, h, i: (b * nq + i, h)),
                  pl.BlockSpec((seq, w), lambda b, h, i: (b, h)),
                  pl.BlockSpec((seq, w), lambda b, h, i: (b, h))],
        out_specs=pl.BlockSpec((blk, w), lambda b, h, i: (b * nq + i, h)),
        out_shape=jax.ShapeDtypeStruct(q.shape, BF16),
        scratch_shapes=[pltpu.VMEM((hb, blk, 1), F32), pltpu.VMEM((blk, w), F32)],
        compiler_params=pltpu.CompilerParams(
            dimension_semantics=("arbitrary", "arbitrary", "arbitrary"),
            vmem_limit_bytes=_vmem_limit(blocks, scratch)),
        name="sb_prompt",
    )(q, k, v)


def _sb_sample_kernel(q_ref, kn_ref, vn_ref, kc_hbm, vc_hbm, o_ref, kbuf, vbuf, carry_ref, acc_ref, u_ref, sem,
                      *, heads, hd, bk, nkb):
    b = pl.program_id(0)
    nq = q_ref.shape[0]
    rows = bk * SUBLANES
    slot = lax.rem(b, 2)
    sls = [slice(h * hd, (h + 1) * hd) for h in range(heads)]

    def fetch(stream, block, buf):
        start = pl.multiple_of((stream * nkb + block) * rows, rows)
        return (pltpu.make_async_copy(kc_hbm.at[pl.ds(start, rows), :], kbuf.at[buf], sem.at[0, buf]),
                pltpu.make_async_copy(vc_hbm.at[pl.ds(start, rows), :], vbuf.at[buf], sem.at[1, buf]))

    def run(ks, vs, carries, accs, umat, mask):
        cs, as_ = _sb_blocks([q_ref[:, sl] for sl in sls], ks, vs, carries, accs, umat, mask)
        for h in range(heads):
            carry_ref[h] = cs[h]
            acc_ref[:, sls[h]] = as_[h]
        return jnp.logical_not(_all_exhausted(cs)).astype(jnp.int32)

    def run_cached(buf):
        return run([kbuf[buf, pl.ds(h, bk, stride=SUBLANES), :].astype(BF16) for h in range(heads)],
                   [vbuf[buf, pl.ds(h, bk, stride=SUBLANES), :].astype(BF16) for h in range(heads)],
                   [carry_ref[h] for h in range(heads)], [acc_ref[:, sl] for sl in sls], u_ref[...], None)

    @pl.when(b == 0)
    def _():
        u_ref[...] = _inclusive_suffix_matrix(bk)
        for copy in fetch(0, nkb - 1, 0):
            copy.start()

    run([kn_ref[:, sl] for sl in sls], [vn_ref[:, sl] for sl in sls],
        [jnp.zeros((nq, 1), F32)] * heads, [jnp.zeros((nq, hd), F32)] * heads,
        _inclusive_suffix_matrix(nq), _strict_causal_mask(nq, nq))

    for copy in fetch(b, nkb - 1, slot):
        copy.wait()

    @pl.when(b + 1 < pl.num_programs(0))
    def _():
        for copy in fetch(b + 1, nkb - 1, 1 - slot):
            copy.start()

    go = run_cached(slot)

    def cond(state):
        j, go = state
        return jnp.logical_and(j >= 0, go > 0)

    def body(state):
        j, _ = state
        copies = fetch(b, j, 2)
        for copy in copies:
            copy.start()
        for copy in copies:
            copy.wait()
        return j - 1, run_cached(2)

    lax.while_loop(cond, body, (jnp.int32(nkb - 2), go))
    o_ref[...] = acc_ref[...].astype(o_ref.dtype)


def _sb_sample(q, kn, vn, kc, vc, batch, nq, past, heads, hd):
    assert heads == SUBLANES and hd == LANES
    width = heads * hd
    bk = min(SB_CACHE_BLOCK, past)
    assert past % bk == 0
    nkb = past // bk
    new_spec = pl.BlockSpec((nq, width), lambda b: (b, 0))
    blocks = 4 * nq * width * 2
    scratch = 2 * 3 * bk * width * 4 + heads * nq * LANES * 4 + nq * width * 4 + 2 * bk * bk * 2
    return pl.pallas_call(
        functools.partial(_sb_sample_kernel, heads=heads, hd=hd, bk=bk, nkb=nkb),
        grid=(batch,),
        in_specs=[new_spec, new_spec, new_spec, pl.BlockSpec(memory_space=pl.ANY), pl.BlockSpec(memory_space=pl.ANY)],
        out_specs=new_spec,
        out_shape=jax.ShapeDtypeStruct(q.shape, BF16),
        scratch_shapes=[pltpu.VMEM((3, bk * SUBLANES, LANES), F32), pltpu.VMEM((3, bk * SUBLANES, LANES), F32),
                        pltpu.VMEM((heads, nq, 1), F32), pltpu.VMEM((nq, width), F32),
                        pltpu.VMEM((2 * bk, bk), BF16), pltpu.SemaphoreType.DMA((2, 3))],
        compiler_params=pltpu.CompilerParams(
            dimension_semantics=("arbitrary",), vmem_limit_bytes=_vmem_limit(blocks, scratch)),
        name="sb_sample",
    )(q, kn, vn, kc, vc)


def _da_steps(ss, vs, m_refs, l_refs, acc_refs):
    n = len(ss)
    nk = ss[0].shape[1]
    m_olds = [r[...] for r in m_refs]
    m_news = [jnp.maximum(m_olds[i], jnp.max(ss[i], axis=1, keepdims=True)) for i in range(n)]
    ps = [jnp.exp2(ss[i] - _widen(m_news[i], nk)) for i in range(n)]
    for i in range(n):
        alpha = jnp.exp2(m_olds[i] - m_news[i])
        l_refs[i][...] = alpha * l_refs[i][...] + jnp.sum(ps[i], axis=1, keepdims=True)
        acc_refs[i][...] = (_widen(alpha, vs[i].shape[1]) * acc_refs[i][...]
                            + _dot(ps[i].astype(BF16), vs[i]))
        m_refs[i][...] = m_news[i]


def _da_same_block_bias(nq, nk, slope2):
    r = lax.broadcasted_iota(jnp.int32, (nq, nk), 0)
    c = lax.broadcasted_iota(jnp.int32, (nq, nk), 1)
    return slope2 * (r - jnp.abs(r - c)).astype(F32)


def _da_chunk_mask(nq, nk):
    r = lax.broadcasted_iota(jnp.int32, (nq, nk), 0)
    c = lax.broadcasted_iota(jnp.int32, (nq, nk), 1)
    return (c // CHUNK) <= (r // CHUNK)


def _da_lambda(lam_ref, lam_init):
    lv = lam_ref[...]
    a = jnp.sum(lv[0:1] * lv[1:2], axis=1, keepdims=True)
    b = jnp.sum(lv[2:3] * lv[3:4], axis=1, keepdims=True)
    return jnp.exp(a) - jnp.exp(b) + lam_init


def _da_finish(acc0, l0, acc1, l1, lam, gain, lam_init):
    vd = acc0.shape[1]
    o = acc0 * _widen(1.0 / l0, vd) - lam * (acc1 * _widen(1.0 / l1, vd))
    y = o * lax.rsqrt(jnp.mean(o * o, axis=1, keepdims=True) + RMS_EPS)
    return (y * gain) * (1.0 - lam_init)


def _da_prompt_kernel(slope_ref, q_ref, k_ref, v_ref, lam_ref, g_ref, o_ref, m_ref, l_ref, acc_ref,
                      *, blk, hd, vd, hb, lam_init):
    hg = pl.program_id(1)
    qi = pl.program_id(2)
    m_ref[...] = jnp.full(m_ref.shape, NEG_INF, F32)
    l_ref[...] = jnp.zeros(l_ref.shape, F32)
    acc_ref[...] = jnp.zeros(acc_ref.shape, F32)
    col = lax.broadcasted_iota(jnp.int32, (1, blk), 1)
    chains = lambda ref: [ref.at[i] for i in range(2 * hb)]
    slopes2 = [slope_ref[hg * hb + h] * LOG2E for h in range(hb)]

    def step(j, bias_of, mask):
        st = pl.multiple_of(j * blk, blk)
        ss, vs = [], []
        for h in range(hb):
            bias = bias_of(slopes2[h])
            v = v_ref[pl.ds(st, blk), h * vd:(h + 1) * vd]
            for m in range(2):
                sl = slice((2 * h + m) * hd, (2 * h + m + 1) * hd)
                s = _dot_nt(q_ref[:, sl], k_ref[pl.ds(st, blk), sl]) + bias
                ss.append(s if mask is None else jnp.where(mask, s, NEG_INF))
                vs.append(v)
        _da_steps(ss, vs, chains(m_ref), chains(l_ref), chains(acc_ref))

    def body(j, c):
        rel = (col + (j - qi) * blk).astype(F32)
        step(j, lambda slope2: slope2 * rel, None)
        return c

    lax.fori_loop(0, qi, body, 0)
    step(qi, lambda slope2: _da_same_block_bias(blk, blk, slope2), _da_chunk_mask(blk, blk))
    lam = _da_lambda(lam_ref, lam_init)
    for h in range(hb):
        o_ref[:, h * vd:(h + 1) * vd] = _da_finish(
            acc_ref[2 * h], l_ref[2 * h], acc_ref[2 * h + 1], l_ref[2 * h + 1],
            lam, g_ref[...], lam_init).astype(o_ref.dtype)


def _da_prompt(q, k, v, slopes, lam_vecs, gain, batch, seq, heads, hd, vd, lam_init):
    blk = min(DA_BLOCK, seq)
    hb = math.gcd(heads, DA_HEADS_PER_STEP)
    assert seq % blk == 0 and blk % CHUNK == 0
    nq = seq // blk
    blocks = hb * (blk * 2 * hd * 2 + blk * vd * 2 + seq * 2 * hd * 2 + seq * vd * 2)
    scratch = hb * (2 * 2 * blk * LANES * 4 + 2 * blk * vd * 4)
    return pl.pallas_call(
        functools.partial(_da_prompt_kernel, blk=blk, hd=hd, vd=vd, hb=hb, lam_init=lam_init),
        grid=(batch, heads // hb, nq),
        in_specs=[pl.BlockSpec(memory_space=pltpu.SMEM),
                  pl.BlockSpec((blk, hb * 2 * hd), lambda b, h, i: (b * nq + i, h)),
                  pl.BlockSpec((seq, hb * 2 * hd), lambda b, h, i: (b, h)),
                  pl.BlockSpec((seq, hb * vd), lambda b, h, i: (b, h)),
                  pl.BlockSpec((4, hd), lambda b, h, i: (0, 0)),
                  pl.BlockSpec((1, vd), lambda b, h, i: (0, 0))],
        out_specs=pl.BlockSpec((blk, hb * vd), lambda b, h, i: (b * nq + i, h)),
        out_shape=jax.ShapeDtypeStruct(v.shape, BF16),
        scratch_shapes=[pltpu.VMEM((2 * hb, blk, LANES), F32), pltpu.VMEM((2 * hb, blk, LANES), F32),
                        pltpu.VMEM((2 * hb, blk, vd), F32)],
        compiler_params=pltpu.CompilerParams(
            dimension_semantics=("arbitrary", "arbitrary", "arbitrary"),
            vmem_limit_bytes=_vmem_limit(blocks, scratch)),
        name="da_prompt",
    )(slopes, q, k, v, lam_vecs, gain)


def _da_sample_kernel(q_ref, kn_ref, vn_ref, kc_ref, vc_ref, lam_ref, g_ref, o_ref,
                      m_ref, l_ref, acc_ref, *, heads, hd, vd, past, slopes, lam_init):
    kb = pl.program_id(1)
    nkb = pl.num_programs(1) - 1
    nq = q_ref.shape[0]
    bk = kc_ref.shape[0] // SUBLANES
    chains = 2 * heads
    refs = lambda ref: [ref.at[i] for i in range(chains)]

    @pl.when(kb == 0)
    def _():
        m_ref[...] = jnp.full(m_ref.shape, NEG_INF, F32)
        l_ref[...] = jnp.zeros(l_ref.shape, F32)
        acc_ref[...] = jnp.zeros(acc_ref.shape, F32)

    def step(k_of, v_of, bias_of):
        ss, vs = [], []
        for h in range(heads):
            v = v_of(h)
            bias = bias_of(slopes[h] * LOG2E)
            for m in range(2):
                i = 2 * h + m
                ss.append(_dot_nt(q_ref[:, i * hd:(i + 1) * hd], k_of(i)) + bias)
                vs.append(v)
        _da_steps(ss, vs, refs(m_ref), refs(l_ref), refs(acc_ref))

    @pl.when(kb < nkb)
    def _():
        col = lax.broadcasted_iota(jnp.int32, (1, bk), 1)
        rel = (col + (kb * bk - past)).astype(F32)
        step(lambda i: _tile_rows(kc_ref, i, bk).astype(BF16),
             lambda h: jnp.concatenate([_tile_rows(vc_ref, half * heads + h, bk) for half in range(vd // LANES)],
                                       axis=1).astype(BF16),
             lambda slope2: slope2 * rel)

    @pl.when(kb == nkb)
    def _():
        step(lambda i: kn_ref[:, i * hd:(i + 1) * hd], lambda h: vn_ref[:, h * vd:(h + 1) * vd],
             lambda slope2: _da_same_block_bias(nq, nq, slope2))
        lam = _da_lambda(lam_ref, lam_init)
        for h in range(heads):
            o_ref[:, h * vd:(h + 1) * vd] = _da_finish(
                acc_ref[2 * h], l_ref[2 * h], acc_ref[2 * h + 1], l_ref[2 * h + 1],
                lam, g_ref[...], lam_init).astype(o_ref.dtype)


def _da_sample(q, kn, vn, kc, vc, slopes, lam_vecs, gain, batch, nq, past, heads, hd, vd, lam_init):
    assert nq == CHUNK and past % CHUNK == 0
    assert hd == LANES and 2 * heads == SUBLANES and heads * (vd // LANES) == SUBLANES
    qk_w = heads * 2 * hd
    v_w = heads * vd
    bk = min(DA_CACHE_BLOCK, past)
    assert past % bk == 0
    nkb = past // bk
    blocks = 2 * nq * qk_w * 2 + 2 * nq * v_w * 2 + bk * (qk_w + v_w) * 4
    scratch = 2 * 2 * heads * nq * LANES * 4 + 2 * heads * nq * vd * 4
    cache_idx = lambda b, j: (b * nkb + jnp.minimum(j, nkb - 1), 0)
    return pl.pallas_call(
        functools.partial(_da_sample_kernel, heads=heads, hd=hd, vd=vd, past=past, slopes=slopes,
                          lam_init=lam_init),
        grid=(batch, nkb + 1),
        in_specs=[pl.BlockSpec((nq, qk_w), lambda b, j: (b, 0)),
                  pl.BlockSpec((nq, qk_w), lambda b, j: (b, 0)),
                  pl.BlockSpec((nq, v_w), lambda b, j: (b, 0)),
                  pl.BlockSpec((bk * SUBLANES, LANES), cache_idx),
                  pl.BlockSpec((bk * SUBLANES, LANES), cache_idx),
                  pl.BlockSpec((4, hd), lambda b, j: (0, 0)),
                  pl.BlockSpec((1, vd), lambda b, j: (0, 0))],
        out_specs=pl.BlockSpec((nq, v_w), lambda b, j: (b, 0)),
        out_shape=jax.ShapeDtypeStruct(vn.shape, BF16),
        scratch_shapes=[pltpu.VMEM((2 * heads, nq, LANES), F32), pltpu.VMEM((2 * heads, nq, LANES), F32),
                        pltpu.VMEM((2 * heads, nq, vd), F32)],
        compiler_params=pltpu.CompilerParams(
            dimension_semantics=("arbitrary", "arbitrary"),
            vmem_limit_bytes=_vmem_limit(blocks, scratch)),
        name="da_sample",
    )(q, kn, vn, kc, vc, lam_vecs, gain)


def _route(logits, n_groups, per_group):
    n_exp = n_groups * per_group
    lane = lax.broadcasted_iota(jnp.int32, logits.shape, 1).astype(F32)
    big = jnp.float32(3e38)

    def first_argmax(x):
        top = jnp.max(x, axis=1, keepdims=True)
        idx = jnp.min(jnp.where(x == top, lane, big), axis=1, keepdims=True)
        return top, idx

    is_coarse = (lane >= n_exp) & (lane < n_exp + n_groups)
    coarse = jnp.where(is_coarse, logits, -big)
    c_top, c_idx = first_argmax(coarse)
    g_gate = 1.0 / jnp.sum(jnp.where(is_coarse, jnp.exp(logits - c_top), 0.0), axis=1, keepdims=True)
    g_first = (c_idx - n_exp) * per_group
    in_group = (lane >= g_first) & (lane < g_first + per_group)
    fine = jnp.where(in_group, logits, -big)
    v1, i1 = first_argmax(fine)
    v2, i2 = first_argmax(jnp.where(lane == i1, -big, fine))
    e2 = jnp.exp(v2 - v1)
    w1 = g_gate / (1.0 + e2)
    w2 = g_gate * e2 / (1.0 + e2)
    comb = jnp.where(lane == i1, w1, jnp.where(lane == i2, w2, 0.0))
    return jnp.where(lane == n_exp, c_idx - n_exp, comb)


def _post_attn_kernel(sb_ref, da_ref, x_ref, wsb_ref, wda_ref, g_ref, b_ref, wr_ref, br_ref,
                      h1_ref, comb_ref, *, alpha, n_groups, per_group):
    mixed = _dot(sb_ref[...], wsb_ref[...]) + _dot(da_ref[...], wda_ref[...])
    h1 = _layer_norm(alpha * x_ref[...] + mixed, g_ref[...], b_ref[...])
    h1_ref[...] = h1
    h_hi, h_lo = _split_bf16(h1)
    both = _dot(h_hi, wr_ref[...])
    logits = both[:, :LANES] + both[:, LANES:] + _dot(h_lo, wr_ref[:, :LANES]) + br_ref[...]
    comb_ref[...] = _route(logits, n_groups, per_group)


def _post_attn(sb_o, da_o, x, w_out, ln_g, ln_b, w_route, b_route, alpha, n_groups, per_group):
    t, d = x.shape
    w_sb = sb_o.shape[1]
    w_da = da_o.shape[1]
    assert w_sb == w_da
    tm = min(TOKEN_TILE, t)
    assert t % tm == 0
    blocks = (tm * (w_sb + w_da) * 2 + tm * d * 4 * 2 + (w_sb + w_da) * d * 2
              + d * 2 * LANES * 2 + tm * LANES * 4)
    row = lambda i: (i, 0)
    fixed = lambda i: (0, 0)
    return pl.pallas_call(
        functools.partial(_post_attn_kernel, alpha=alpha, n_groups=n_groups, per_group=per_group),
        grid=(t // tm,),
        in_specs=[pl.BlockSpec((tm, w_sb), row), pl.BlockSpec((tm, w_da), row), pl.BlockSpec((tm, d), row),
                  pl.BlockSpec((w_sb, d), fixed), pl.BlockSpec((w_da, d), lambda i: (1, 0)),
                  pl.BlockSpec((1, d), fixed), pl.BlockSpec((1, d), fixed),
                  pl.BlockSpec((d, 2 * LANES), fixed), pl.BlockSpec((1, LANES), fixed)],
        out_specs=[pl.BlockSpec((tm, d), row), pl.BlockSpec((tm, LANES), row)],
        out_shape=[jax.ShapeDtypeStruct((t, d), F32), jax.ShapeDtypeStruct((t, LANES), F32)],
        compiler_params=pltpu.CompilerParams(
            dimension_semantics=("arbitrary",), vmem_limit_bytes=_vmem_limit(blocks)),
        name="post_attn",
    )(sb_o, da_o, x, w_out, w_out, ln_g, ln_b, w_route, b_route)


def _dispatch_plan(comb, n_exp, n_groups, tm):
    t = comb.shape[0]
    per_group = n_exp // n_groups
    n_tiles = t // tm + n_groups
    group = comb[:, n_exp].astype(jnp.int32)
    active = comb[:, :n_exp] != 0.0
    active_bits = jnp.sum(active.astype(jnp.int32) << jnp.arange(n_exp, dtype=jnp.int32)[None, :], axis=1)
    counts = jnp.sum((group[:, None] == jnp.arange(n_groups)[None, :]).astype(jnp.int32), axis=0)
    tiles_g = (counts + tm - 1) // tm
    tile_end = jnp.cumsum(tiles_g)
    tile_start = tile_end - tiles_g
    count_start = jnp.cumsum(counts) - counts
    tile_ids = jnp.arange(n_tiles, dtype=jnp.int32)
    tile_group = jnp.minimum(jnp.sum((tile_ids[:, None] >= tile_end[None, :]).astype(jnp.int32), axis=1),
                             n_groups - 1)
    first_row = (tile_ids - tile_start[tile_group]) * tm
    tile_valid = jnp.clip(counts[tile_group] - first_row, 0, tm)
    order = jnp.argsort(group * (1 << n_exp) + active_bits, stable=True).astype(jnp.int32)
    slot_group = jnp.repeat(tile_group, tm)
    rank = jnp.repeat(first_row, tm) + jnp.tile(jnp.arange(tm, dtype=jnp.int32), n_tiles)
    rank = jnp.minimum(rank, counts[slot_group] - 1)
    src = order[jnp.clip(count_start[slot_group] + rank, 0, t - 1)]
    comb_sorted = comb[src]
    tile_active = jnp.any((comb_sorted[:, :n_exp] != 0.0).reshape(n_tiles, tm, n_exp), axis=1)
    in_group = jnp.take_along_axis(
        tile_active, tile_group[:, None] * per_group + jnp.arange(per_group)[None, :], axis=1)
    in_group = jnp.logical_and(in_group, (tile_ids < tile_end[-1])[:, None])
    first_needed = jnp.argsort(jnp.logical_not(in_group), axis=1, stable=True)
    need = (jnp.arange(per_group)[None, :] < jnp.sum(in_group.astype(jnp.int32), axis=1)[:, None]).reshape(-1)
    step_expert = (tile_group[:, None] * per_group + first_needed).reshape(-1)
    steps = jnp.arange(n_tiles * per_group, dtype=jnp.int32)
    last_needed = lax.cummax(jnp.where(need, steps, -1), axis=0)
    resident = step_expert[jnp.maximum(last_needed, 0)]
    return (src, tile_valid.astype(jnp.int32), tile_end[-1:].astype(jnp.int32),
            need.astype(jnp.int32), resident.astype(jnp.int32), comb_sorted)


def _moe_kernel(src_ref, valid_ref, used_ref, need_ref, resident_ref,
                h1_hbm, comb_ref, wg_ref, wu_ref, wd_ref, g_ref, b_ref,
                y_hbm, xbuf, xb_ref, acc_ref, obuf, sem_in, sem_out, *, alpha, tm, per_group):
    i = pl.program_id(0)
    j = pl.program_id(1)
    n_used = used_ref[0]
    slot = lax.rem(i, 2)
    live = i < n_used

    def row_in(tile, r, buf_slot):
        return pltpu.make_async_copy(h1_hbm.at[pl.ds(src_ref[tile * tm + r], 1), :],
                                     xbuf.at[buf_slot, pl.ds(r, 1), :], sem_in.at[buf_slot])

    def row_out(tile, r):
        return pltpu.make_async_copy(obuf.at[pl.ds(r, 1), :],
                                     y_hbm.at[pl.ds(src_ref[tile * tm + r], 1), :], sem_out)

    def gather(tile, buf_slot):
        def issue(r, c):
            row_in(tile, r, buf_slot).start()
            return c
        lax.fori_loop(0, tm, issue, 0, unroll=8)

    def wait_gather(buf_slot):
        pltpu.make_async_copy(h1_hbm.at[pl.ds(0, tm), :], xbuf.at[buf_slot], sem_in.at[buf_slot]).wait()

    def drain_scatter(tile):
        n = valid_ref[tile]

        def wait_chunk(c, carry):
            pltpu.make_async_copy(obuf.at[pl.ds(0, SCATTER_WAIT_ROWS), :],
                                  y_hbm.at[pl.ds(0, SCATTER_WAIT_ROWS), :], sem_out).wait()
            return carry

        def wait_row(r, carry):
            row_out(tile, 0).wait()
            return carry
        lax.fori_loop(0, n // SCATTER_WAIT_ROWS, wait_chunk, 0)
        lax.fori_loop(0, lax.rem(n, SCATTER_WAIT_ROWS), wait_row, 0)

    def expert_step():
        xb = xb_ref[...]
        gate = _dot(xb, wg_ref[...])
        up = _dot(xb, wu_ref[...])
        hidden = (gate / (1.0 + jnp.exp(-gate))) * up
        lane = lax.broadcasted_iota(jnp.int32, comb_ref.shape, 1)
        expert = resident_ref[i * per_group + j]
        weight = jnp.sum(jnp.where(lane == expert, comb_ref[...], 0.0), axis=1, keepdims=True)
        return weight * _dot(hidden.astype(BF16), wd_ref[...])

    @pl.when(jnp.logical_and(live, j == 0))
    def _():
        @pl.when(i == 0)
        def _():
            gather(0, 0)
        wait_gather(slot)
        xb_ref[...] = xbuf[slot].astype(BF16)
        nxt = jnp.minimum(i + 1, n_used - 1)
        for r in range(tm):
            row_in(nxt, r, 1 - slot).start()
        acc_ref[...] = expert_step()

    @pl.when(jnp.logical_and(jnp.logical_and(live, j > 0), need_ref[i * per_group + j] > 0))
    def _():
        acc_ref[...] += expert_step()

    @pl.when(jnp.logical_and(live, j == per_group - 1))
    def _():
        @pl.when(i > 0)
        def _():
            drain_scatter(i - 1)
        obuf[...] = _layer_norm(alpha * xbuf[slot] + acc_ref[...], g_ref[...], b_ref[...])

        n = valid_ref[i]

        def issue_group(c, carry):
            for r in range(SCATTER_UNROLL):
                row_out(i, c * SCATTER_UNROLL + r).start()
            return carry

        def issue_row(r, carry):
            row_out(i, (n // SCATTER_UNROLL) * SCATTER_UNROLL + r).start()
            return carry
        lax.fori_loop(0, n // SCATTER_UNROLL, issue_group, 0)
        lax.fori_loop(0, lax.rem(n, SCATTER_UNROLL), issue_row, 0)

        @pl.when(i == n_used - 1)
        def _():
            drain_scatter(i)
            wait_gather(1 - slot)


def _moe(h1, comb, w_gate, w_up, w_down, ln_g, ln_b, alpha, n_groups):
    t, d = h1.shape
    n_exp, _, d_e = w_gate.shape
    per_group = n_exp // n_groups
    tm = min(TOKEN_TILE if t >= 4 * n_groups * TOKEN_TILE else TOKEN_TILE // 2, t)
    assert t % tm == 0
    src, tile_valid, n_used, need, resident, comb_sorted = _dispatch_plan(comb, n_exp, n_groups, tm)
    n_tiles = tile_valid.shape[0]
    blocks = tm * LANES * 4 + 3 * d * d_e * 2
    scratch = 2 * tm * d * 4 + tm * d * 2 + 2 * tm * d * 4
    expert_idx = lambda i, j, src, valid, used, need, res: (res[i * per_group + j], 0, 0)
    fixed = lambda i, j, *prefetch: (0, 0)
    return pl.pallas_call(
        functools.partial(_moe_kernel, alpha=alpha, tm=tm, per_group=per_group),
        grid_spec=pltpu.PrefetchScalarGridSpec(
            num_scalar_prefetch=5,
            grid=(n_tiles, per_group),
            in_specs=[pl.BlockSpec(memory_space=pl.ANY),
                      pl.BlockSpec((tm, LANES), lambda i, j, *prefetch: (i, 0)),
                      pl.BlockSpec((None, d, d_e), expert_idx),
                      pl.BlockSpec((None, d, d_e), expert_idx),
                      pl.BlockSpec((None, d_e, d), expert_idx),
                      pl.BlockSpec((1, d), fixed), pl.BlockSpec((1, d), fixed)],
            out_specs=pl.BlockSpec(memory_space=pl.ANY),
            scratch_shapes=[pltpu.VMEM((2, tm, d), F32), pltpu.VMEM((tm, d), BF16), pltpu.VMEM((tm, d), F32),
                            pltpu.VMEM((tm, d), F32), pltpu.SemaphoreType.DMA((2,)), pltpu.SemaphoreType.DMA]),
        out_shape=jax.ShapeDtypeStruct((t, d), F32),
        compiler_params=pltpu.CompilerParams(
            dimension_semantics=("arbitrary", "arbitrary"),
            vmem_limit_bytes=_vmem_limit(blocks, scratch)),
        name="moe",
    )(src, tile_valid, n_used, need, resident, h1, comb_sorted, w_gate, w_up, w_down, ln_g, ln_b)


def kernel(x_prompt, x_sample, cache_sb_k, cache_sb_v, cache_da_k, cache_da_v, w_in, w_out, lambda_q1, lambda_k1, lambda_q2, lambda_k2, subln_g, ln1_g, ln1_b, w_coarse, b_coarse, w_fine, b_fine, w_gate, w_up, w_down, ln2_g, ln2_b):
    depth, dec_batch, past, sb_heads, sb_hd = cache_sb_k.shape
    _, _, _, da_heads, _, da_hd = cache_da_k.shape
    da_vd = cache_da_v.shape[-1]
    batch, seq, d_model = x_prompt.shape
    dec_seq = x_sample.shape[1]
    n_groups, per_group = w_fine.shape[2], w_fine.shape[3]
    n_exp = n_groups * per_group
    assert n_exp + n_groups <= LANES
    sb_w = sb_heads * sb_hd
    assert sb_w == da_heads * 2 * da_hd == da_heads * da_vd == SUBLANES * LANES, \
        "projection column groups are addressed as equal-width blocks of one token tile"
    alpha = (2 * depth) ** 0.25
    slopes = tuple(2.0 ** (-8.0 * (h + 1) / da_heads) for h in range(da_heads))
    slopes_arr = jnp.asarray(slopes, F32)

    hp = x_prompt.reshape(batch * seq, d_model)
    hs = x_sample.reshape(dec_batch * dec_seq, d_model)
    rows_p, rows_s = [], []
    for l in range(depth):
        lam_init = 0.8 - 0.6 * math.exp(-0.3 * l)
        w_in_b = w_in[l].astype(BF16)
        w_out_b = w_out[l].astype(BF16)
        lam_vecs = jnp.stack([lambda_q1[l], lambda_k1[l], lambda_q2[l], lambda_k2[l]]).astype(F32)
        gain = subln_g[l].reshape(1, da_vd)
        w_route = jnp.concatenate(
            [w_fine[l].reshape(d_model, n_exp), w_coarse[l],
             jnp.zeros((d_model, LANES - n_exp - n_groups), F32)], axis=1)
        w_route_b = jnp.concatenate(_split_bf16(w_route), axis=1)
        b_route = jnp.concatenate(
            [b_fine[l].reshape(n_exp), b_coarse[l], jnp.zeros((LANES - n_exp - n_groups,), F32)]).reshape(1, LANES)
        wg_b, wu_b, wd_b = w_gate[l].astype(BF16), w_up[l].astype(BF16), w_down[l].astype(BF16)
        g1, b1 = ln1_g[l].reshape(1, d_model), ln1_b[l].reshape(1, d_model)
        g2, b2 = ln2_g[l].reshape(1, d_model), ln2_b[l].reshape(1, d_model)

        halves = da_vd // LANES
        dv_slots = tuple(h * halves + half for half in range(halves) for h in range(da_heads))
        plain_slots = tuple(range(SUBLANES))

        sb_scale = sb_hd ** -0.5 * LOG2E
        da_scale = da_hd ** -0.5 * LOG2E

        def project(x, tag):
            sq, sk_b, sv_b, sk, sv = _proj(
                x, w_in_b, 0, ((sb_scale, None), (None, plain_slots), (None, plain_slots)), "proj_sb_" + tag)
            dq, dk_b, dv_b, dk, dv = _proj(
                x, w_in_b, 3, ((da_scale, None), (None, plain_slots), (None, dv_slots)), "proj_da_" + tag)
            return (sq, sk_b, sv_b, dq, dk_b, dv_b), (sk, sv, dk, dv)

        def dv_tiles_to_rows(tiles, lead):
            return tiles.reshape(lead + (halves, da_heads, LANES)).swapaxes(-3, -2).reshape(lead + (da_heads, da_vd))

        def dv_rows_to_tiles(rows_, n):
            return rows_.reshape(n, da_heads, halves, LANES).swapaxes(1, 2).reshape(n * SUBLANES, LANES)

        (sq, sk, sv, dq, dk, dv), new_p = project(hp, "p")
        sb_o = _sb_prompt(sq, sk, sv, batch, seq, sb_heads, sb_hd)
        da_o = _da_prompt(dq, dk, dv, slopes_arr, lam_vecs, gain, batch, seq, da_heads, da_hd, da_vd, lam_init)
        h1, comb = _post_attn(sb_o, da_o, hp, w_out_b, g1, b1, w_route_b, b_route, alpha, n_groups, per_group)
        hp = _moe(h1, comb, wg_b, wu_b, wd_b, g2, b2, alpha, n_groups)

        (sq, sk, sv, dq, dk, dv), new_s = project(hs, "s")
        rows = dec_batch * past
        sb_o = _sb_sample(sq, sk, sv, cache_sb_k[l].reshape(rows * SUBLANES, LANES),
                          cache_sb_v[l].reshape(rows * SUBLANES, LANES),
                          dec_batch, dec_seq, past, sb_heads, sb_hd)
        da_o = _da_sample(dq, dk, dv, cache_da_k[l].reshape(rows * SUBLANES, LANES),
                          dv_rows_to_tiles(cache_da_v[l], rows),
                          slopes, lam_vecs, gain, dec_batch, dec_seq, past, da_heads, da_hd, da_vd, lam_init)
        h1, comb = _post_attn(sb_o, da_o, hs, w_out_b, g1, b1, w_route_b, b_route, alpha, n_groups, per_group)
        hs = _moe(h1, comb, wg_b, wu_b, wd_b, g2, b2, alpha, n_groups)
        for new, lead, rows_out in ((new_p, (batch, seq), rows_p), (new_s, (dec_batch, dec_seq), rows_s)):
            rows_out.append((new[0].reshape(lead + (sb_heads, sb_hd)), new[1].reshape(lead + (sb_heads, sb_hd)),
                             new[2].reshape(lead + (da_heads, 2, da_hd)), dv_tiles_to_rows(new[3], lead)))

    def stack(rows, i):
        return rows[0][i][None] if len(rows) == 1 else jnp.stack([r[i] for r in rows], axis=0)

    return (hp.reshape(batch, seq, d_model), hs.reshape(dec_batch, dec_seq, d_model),
            *(stack(rows_p, i) for i in range(4)), *(stack(rows_s, i) for i in range(4)))
```

```python
import functools
import math

import jax
import jax.numpy as jnp
from jax import lax
from jax.experimental import pallas as pl
from jax.experimental.pallas import tpu as pltpu

F32 = jnp.float32
BF16 = jnp.bfloat16

CHUNK = 64
LN_EPS = 1e-5
RMS_EPS = 1e-5
NEG_INF = -1e30
LOG2E = 1.4426950408889634
LANES = 128
SUBLANES = 8
VMEM_CAP = 64 * 2**20
SB_SKIP_LOG2 = 150.0
SCATTER_UNROLL = 8
SCATTER_WAIT_ROWS = 64
TOKEN_TILE = 512
SB_BLOCK = 256
SB_HEADS_PER_STEP = 4
DA_BLOCK = 512
DA_HEADS_PER_STEP = 2
SB_CACHE_BLOCK = 512
DA_CACHE_BLOCK = 1024


def _vmem_limit(block_bytes, scratch_bytes=0):
    est = 2 * block_bytes + scratch_bytes + 16 * 2**20
    return int(min(est, VMEM_CAP - 6 * 2**20))


def _dot(a, b):
    return jnp.dot(a, b, preferred_element_type=F32)


def _dot_nt(a, b):
    return lax.dot_general(a, b, (((1,), (1,)), ((), ())), preferred_element_type=F32)


def _widen(x, width):
    if width < LANES:
        return x[:, :width]
    return x if width == LANES else jnp.concatenate([x] * (width // LANES), axis=1)


def _split_bf16(x):
    hi = x.astype(BF16)
    return hi, (x - hi.astype(F32)).astype(BF16)


def _layer_norm(y, g, b):
    mu = jnp.mean(y, axis=1, keepdims=True)
    yc = y - mu
    var = jnp.mean(yc * yc, axis=1, keepdims=True)
    return yc * lax.rsqrt(var + LN_EPS) * g + b


def _tile_rows(ref, slot, n):
    return ref[pl.ds(slot, n, stride=SUBLANES), :]


def _proj_kernel(x_ref, w_ref, *out_refs, groups):
    xb = x_ref[...].astype(BF16)
    width = SUBLANES * LANES
    o32_refs = iter(out_refs[len(groups):])
    for g, (scale, slots) in enumerate(groups):
        acc = _dot(xb, w_ref[:, g * width:(g + 1) * width])
        out_refs[g][...] = (acc if scale is None else acc * scale).astype(BF16)
        if slots:
            o32_ref = next(o32_refs)
            for j, cb in enumerate(slots):
                o32_ref[pl.ds(j, acc.shape[0], stride=SUBLANES), :] = acc[:, cb * LANES:(cb + 1) * LANES]


def _proj(x, w, first_group, groups, name):
    t, d = x.shape
    width = SUBLANES * LANES
    n = len(groups)
    assert first_group % n == 0
    tm = min(TOKEN_TILE, t)
    assert t % tm == 0
    n32 = sum(1 for _, slots in groups if slots)
    blk = tm * d * 4 + d * n * width * 2 + tm * width * (2 * n + 4 * n32)
    out_specs = ([pl.BlockSpec((tm, width), lambda i: (i, 0))] * n
                 + [pl.BlockSpec((tm * SUBLANES, LANES), lambda i: (i, 0))] * n32)
    out_shape = ([jax.ShapeDtypeStruct((t, width), BF16)] * n
                 + [jax.ShapeDtypeStruct((t * SUBLANES, LANES), F32)] * n32)
    return pl.pallas_call(
        functools.partial(_proj_kernel, groups=groups),
        grid=(t // tm,),
        in_specs=[pl.BlockSpec((tm, d), lambda i: (i, 0)),
                  pl.BlockSpec((d, n * width), lambda i: (0, first_group // n))],
        out_specs=out_specs,
        out_shape=out_shape,
        compiler_params=pltpu.CompilerParams(
            dimension_semantics=("arbitrary",), vmem_limit_bytes=_vmem_limit(blk)),
        name=name,
    )(x, w)


def _inclusive_suffix_matrix(n):
    r = lax.broadcasted_iota(jnp.int32, (2 * n, n), 0)
    c = lax.broadcasted_iota(jnp.int32, (2 * n, n), 1)
    return (jnp.where(r >= n, r - n, r) >= c).astype(BF16)


def _strict_causal_mask(nq, nk):
    r = lax.broadcasted_iota(jnp.int32, (nq, nk), 0)
    c = lax.broadcasted_iota(jnp.int32, (nq, nk), 1)
    return c < r


def _sb_blocks(qs, ks, vs, carries, accs, umat, mask):
    n = len(qs)
    nq = qs[0].shape[0]
    zs = [_dot_nt(qs[i], ks[i]) for i in range(n)]
    splits = []
    for z in zs:
        neg_abs = lax.bitcast_convert_type(
            lax.bitcast_convert_type(z, jnp.uint32) | jnp.uint32(0x80000000), F32)
        sp = jnp.maximum(z, 0.0) + jnp.log(1.0 + jnp.exp2(neg_abs)) * LOG2E
        if mask is not None:
            sp = jnp.where(mask, sp, 0.0)
        splits.append(jnp.concatenate(_split_bf16(sp), axis=1))
    incl_all = _dot(jnp.concatenate(splits, axis=0) if n > 1 else splits[0], umat)
    out_c, out_a = [], []
    for i in range(n):
        incl = incl_all[i * nq:(i + 1) * nq]
        w = jnp.exp2(zs[i] - incl - carries[i])
        if mask is not None:
            w = jnp.where(mask, w, 0.0)
        out_a.append(accs[i] + _dot(w.astype(BF16), vs[i]))
        out_c.append(carries[i] + incl[:, :1])
    return out_c, out_a


def _all_exhausted(carries):
    lowest = functools.reduce(jnp.minimum, carries)
    return jnp.min(lowest) >= SB_SKIP_LOG2


def _sb_prompt_kernel(q_ref, k_ref, v_ref, o_ref, carry_ref, acc_ref, *, blk, hd):
    qi = pl.program_id(2)
    nh = q_ref.shape[1] // hd
    umat = _inclusive_suffix_matrix(blk)
    sls = [slice(h * hd, (h + 1) * hd) for h in range(nh)]

    def run(start, carries, accs, mask):
        cs, as_ = _sb_blocks([q_ref[:, sl] for sl in sls], [k_ref[pl.ds(start, blk), sl] for sl in sls],
                             [v_ref[pl.ds(start, blk), sl] for sl in sls], carries, accs, umat, mask)
        for h in range(nh):
            carry_ref[h] = cs[h]
            acc_ref[:, sls[h]] = as_[h]
        return cs

    run(pl.multiple_of(qi * blk, blk), [jnp.zeros((blk, 1), F32)] * nh, [jnp.zeros((blk, hd), F32)] * nh,
        _strict_causal_mask(blk, blk))

    def cond(state):
        j, go = state
        return jnp.logical_and(j >= 0, go > 0)

    def body(state):
        j, _ = state
        cs = run(pl.multiple_of(j * blk, blk), [carry_ref[h] for h in range(nh)],
                 [acc_ref[:, sl] for sl in sls], None)
        return j - 1, jnp.logical_not(_all_exhausted(cs)).astype(jnp.int32)

    lax.while_loop(cond, body, (qi - 1, jnp.int32(1)))
    o_ref[...] = acc_ref[...].astype(o_ref.dtype)


def _sb_prompt(q, k, v, batch, seq, heads, hd):
    blk = min(SB_BLOCK, seq)
    hb = math.gcd(heads, SB_HEADS_PER_STEP)
    assert seq % blk == 0
    nq = seq // blk
    w = hb * hd
    blocks = blk * w * 2 * 2 + 2 * seq * w * 2
    scratch = hb * blk * LANES * 4 + blk * w * 4
    return pl.pallas_call(
        functools.partial(_sb_prompt_kernel, blk=blk, hd=hd),
        grid=(batch, heads // hb, nq),
        in_specs=[pl.BlockSpec((blk, w), lambda b, h, i: (b * nq + i, h)),
                  pl.BlockSpec((seq, w), lambda b, h, i: (b, h)),
                  pl.BlockSpec((seq, w), lambda b, h, i: (b, h))],
        out_specs=pl.BlockSpec((blk, w), lambda b, h, i: (b * nq + i, h)),
        out_shape=jax.ShapeDtypeStruct(q.shape, BF16),
        scratch_shapes=[pltpu.VMEM((hb, blk, 1), F32), pltpu.VMEM((blk, w), F32)],
        compiler_params=pltpu.CompilerParams(
            dimension_semantics=("arbitrary", "arbitrary", "arbitrary"),
            vmem_limit_bytes=_vmem_limit(blocks, scratch)),
        name="sb_prompt",
    )(q, k, v)


def _sb_sample_kernel(q_ref, kn_ref, vn_ref, kc_hbm, vc_hbm, o_ref, kbuf, vbuf, carry_ref, acc_ref, u_ref, sem,
                      *, heads, hd, bk, nkb):
    b = pl.program_id(0)
    nq = q_ref.shape[0]
    rows = bk * SUBLANES
    slot = lax.rem(b, 2)
    sls = [slice(h * hd, (h + 1) * hd) for h in range(heads)]

    def fetch(stream, block, buf):
        start = pl.multiple_of((stream * nkb + block) * rows, rows)
        return (pltpu.make_async_copy(kc_hbm.at[pl.ds(start, rows), :], kbuf.at[buf], sem.at[0, buf]),
                pltpu.make_async_copy(vc_hbm.at[pl.ds(start, rows), :], vbuf.at[buf], sem.at[1, buf]))

    def run(ks, vs, carries, accs, umat, mask):
        cs, as_ = _sb_blocks([q_ref[:, sl] for sl in sls], ks, vs, carries, accs, umat, mask)
        for h in range(heads):
            carry_ref[h] = cs[h]
            acc_ref[:, sls[h]] = as_[h]
        return jnp.logical_not(_all_exhausted(cs)).astype(jnp.int32)

    def run_cached(buf):
        return run([kbuf[buf, pl.ds(h, bk, stride=SUBLANES), :].astype(BF16) for h in range(heads)],
                   [vbuf[buf, pl.ds(h, bk, stride=SUBLANES), :].astype(BF16) for h in range(heads)],
                   [carry_ref[h] for h in range(heads)], [acc_ref[:, sl] for sl in sls], u_ref[...], None)

    @pl.when(b == 0)
    def _():
        u_ref[...] = _inclusive_suffix_matrix(bk)
        for copy in fetch(0, nkb - 1, 0):
            copy.start()

    run([kn_ref[:, sl] for sl in sls], [vn_ref[:, sl] for sl in sls],
        [jnp.zeros((nq, 1), F32)] * heads, [jnp.zeros((nq, hd), F32)] * heads,
        _inclusive_suffix_matrix(nq), _strict_causal_mask(nq, nq))

    for copy in fetch(b, nkb - 1, slot):
        copy.wait()

    @pl.when(b + 1 < pl.num_programs(0))
    def _():
        for copy in fetch(b + 1, nkb - 1, 1 - slot):
            copy.start()

    go = run_cached(slot)

    def cond(state):
        j, go = state
        return jnp.logical_and(j >= 0, go > 0)

    def body(state):
        j, _ = state
        copies = fetch(b, j, 2)
        for copy in copies:
            copy.start()
        for copy in copies:
            copy.wait()
        return j - 1, run_cached(2)

    lax.while_loop(cond, body, (jnp.int32(nkb - 2), go))
    o_ref[...] = acc_ref[...].astype(o_ref.dtype)


def _sb_sample(q, kn, vn, kc, vc, batch, nq, past, heads, hd):
    assert heads == SUBLANES and hd == LANES
    width = heads * hd
    bk = min(SB_CACHE_BLOCK, past)
    assert past % bk == 0
    nkb = past // bk
    new_spec = pl.BlockSpec((nq, width), lambda b: (b, 0))
    blocks = 4 * nq * width * 2
    scratch = 2 * 3 * bk * width * 4 + heads * nq * LANES * 4 + nq * width * 4 + 2 * bk * bk * 2
    return pl.pallas_call(
        functools.partial(_sb_sample_kernel, heads=heads, hd=hd, bk=bk, nkb=nkb),
        grid=(batch,),
        in_specs=[new_spec, new_spec, new_spec, pl.BlockSpec(memory_space=pl.ANY), pl.BlockSpec(memory_space=pl.ANY)],
        out_specs=new_spec,
        out_shape=jax.ShapeDtypeStruct(q.shape, BF16),
        scratch_shapes=[pltpu.VMEM((3, bk * SUBLANES, LANES), F32), pltpu.VMEM((3, bk * SUBLANES, LANES), F32),
                        pltpu.VMEM((heads, nq, 1), F32), pltpu.VMEM((nq, width), F32),
                        pltpu.VMEM((2 * bk, bk), BF16), pltpu.SemaphoreType.DMA((2, 3))],
        compiler_params=pltpu.CompilerParams(
            dimension_semantics=("arbitrary",), vmem_limit_bytes=_vmem_limit(blocks, scratch)),
        name="sb_sample",
    )(q, kn, vn, kc, vc)


def _da_steps(ss, vs, m_refs, l_refs, acc_refs):
    n = len(ss)
    nk = ss[0].shape[1]
    m_olds = [r[...] for r in m_refs]
    m_news = [jnp.maximum(m_olds[i], jnp.max(ss[i], axis=1, keepdims=True)) for i in range(n)]
    ps = [jnp.exp2(ss[i] - _widen(m_news[i], nk)) for i in range(n)]
    for i in range(n):
        alpha = jnp.exp2(m_olds[i] - m_news[i])
        l_refs[i][...] = alpha * l_refs[i][...] + jnp.sum(ps[i], axis=1, keepdims=True)
        acc_refs[i][...] = (_widen(alpha, vs[i].shape[1]) * acc_refs[i][...]
                            + _dot(ps[i].astype(BF16), vs[i]))
        m_refs[i][...] = m_news[i]


def _da_same_block_bias(nq, nk, slope2):
    r = lax.broadcasted_iota(jnp.int32, (nq, nk), 0)
    c = lax.broadcasted_iota(jnp.int32, (nq, nk), 1)
    return slope2 * (r - jnp.abs(r - c)).astype(F32)


def _da_chunk_mask(nq, nk):
    r = lax.broadcasted_iota(jnp.int32, (nq, nk), 0)
    c = lax.broadcasted_iota(jnp.int32, (nq, nk), 1)
    return (c // CHUNK) <= (r // CHUNK)


def _da_lambda(lam_ref, lam_init):
    lv = lam_ref[...]
    a = jnp.sum(lv[0:1] * lv[1:2], axis=1, keepdims=True)
    b = jnp.sum(lv[2:3] * lv[3:4], axis=1, keepdims=True)
    return jnp.exp(a) - jnp.exp(b) + lam_init


def _da_finish(acc0, l0, acc1, l1, lam, gain, lam_init):
    vd = acc0.shape[1]
    o = acc0 * _widen(1.0 / l0, vd) - lam * (acc1 * _widen(1.0 / l1, vd))
    y = o * lax.rsqrt(jnp.mean(o * o, axis=1, keepdims=True) + RMS_EPS)
    return (y * gain) * (1.0 - lam_init)


def _da_prompt_kernel(slope_ref, q_ref, k_ref, v_ref, lam_ref, g_ref, o_ref, m_ref, l_ref, acc_ref,
                      *, blk, hd, vd, hb, lam_init):
    hg = pl.program_id(1)
    qi = pl.program_id(2)
    m_ref[...] = jnp.full(m_ref.shape, NEG_INF, F32)
    l_ref[...] = jnp.zeros(l_ref.shape, F32)
    acc_ref[...] = jnp.zeros(acc_ref.shape, F32)
    col = lax.broadcasted_iota(jnp.int32, (1, blk), 1)
    chains = lambda ref: [ref.at[i] for i in range(2 * hb)]
    slopes2 = [slope_ref[hg * hb + h] * LOG2E for h in range(hb)]

    def step(j, bias_of, mask):
        st = pl.multiple_of(j * blk, blk)
        ss, vs = [], []
        for h in range(hb):
            bias = bias_of(slopes2[h])
            v = v_ref[pl.ds(st, blk), h * vd:(h + 1) * vd]
            for m in range(2):
                sl = slice((2 * h + m) * hd, (2 * h + m + 1) * hd)
                s = _dot_nt(q_ref[:, sl], k_ref[pl.ds(st, blk), sl]) + bias
                ss.append(s if mask is None else jnp.where(mask, s, NEG_INF))
                vs.append(v)
        _da_steps(ss, vs, chains(m_ref), chains(l_ref), chains(acc_ref))

    def body(j, c):
        rel = (col + (j - qi) * blk).astype(F32)
        step(j, lambda slope2: slope2 * rel, None)
        return c

    lax.fori_loop(0, qi, body, 0)
    step(qi, lambda slope2: _da_same_block_bias(blk, blk, slope2), _da_chunk_mask(blk, blk))
    lam = _da_lambda(lam_ref, lam_init)
    for h in range(hb):
        o_ref[:, h * vd:(h + 1) * vd] = _da_finish(
            acc_ref[2 * h], l_ref[2 * h], acc_ref[2 * h + 1], l_ref[2 * h + 1],
            lam, g_ref[...], lam_init).astype(o_ref.dtype)


def _da_prompt(q, k, v, slopes, lam_vecs, gain, batch, seq, heads, hd, vd, lam_init):
    blk = min(DA_BLOCK, seq)
    hb = math.gcd(heads, DA_HEADS_PER_STEP)
    assert seq % blk == 0 and blk % CHUNK == 0
    nq = seq // blk
    blocks = hb * (blk * 2 * hd * 2 + blk * vd * 2 + seq * 2 * hd * 2 + seq * vd * 2)
    scratch = hb * (2 * 2 * blk * LANES * 4 + 2 * blk * vd * 4)
    return pl.pallas_call(
        functools.partial(_da_prompt_kernel, blk=blk, hd=hd, vd=vd, hb=hb, lam_init=lam_init),
        grid=(batch, heads // hb, nq),
        in_specs=[pl.BlockSpec(memory_space=pltpu.SMEM),
                  pl.BlockSpec((blk, hb * 2 * hd), lambda b, h, i: (b * nq + i, h)),
                  pl.BlockSpec((seq, hb * 2 * hd), lambda b, h, i: (b, h)),
                  pl.BlockSpec((seq, hb * vd), lambda b, h, i: (b, h)),
                  pl.BlockSpec((4, hd), lambda b, h, i: (0, 0)),
                  pl.BlockSpec((1, vd), lambda b, h, i: (0, 0))],
        out_specs=pl.BlockSpec((blk, hb * vd), lambda b, h, i: (b * nq + i, h)),
        out_shape=jax.ShapeDtypeStruct(v.shape, BF16),
        scratch_shapes=[pltpu.VMEM((2 * hb, blk, LANES), F32), pltpu.VMEM((2 * hb, blk, LANES), F32),
                        pltpu.VMEM((2 * hb, blk, vd), F32)],
        compiler_params=pltpu.CompilerParams(
            dimension_semantics=("arbitrary", "arbitrary", "arbitrary"),
            vmem_limit_bytes=_vmem_limit(blocks, scratch)),
        name="da_prompt",
    )(slopes, q, k, v, lam_vecs, gain)


def _da_sample_kernel(q_ref, kn_ref, vn_ref, kc_ref, vc_ref, lam_ref, g_ref, o_ref,
                      m_ref, l_ref, acc_ref, *, heads, hd, vd, past, slopes, lam_init):
    kb = pl.program_id(1)
    nkb = pl.num_programs(1) - 1
    nq = q_ref.shape[0]
    bk = kc_ref.shape[0] // SUBLANES
    chains = 2 * heads
    refs = lambda ref: [ref.at[i] for i in range(chains)]

    @pl.when(kb == 0)
    def _():
        m_ref[...] = jnp.full(m_ref.shape, NEG_INF, F32)
        l_ref[...] = jnp.zeros(l_ref.shape, F32)
        acc_ref[...] = jnp.zeros(acc_ref.shape, F32)

    def step(k_of, v_of, bias_of):
        ss, vs = [], []
        for h in range(heads):
            v = v_of(h)
            bias = bias_of(slopes[h] * LOG2E)
            for m in range(2):
                i = 2 * h + m
                ss.append(_dot_nt(q_ref[:, i * hd:(i + 1) * hd], k_of(i)) + bias)
                vs.append(v)
        _da_steps(ss, vs, refs(m_ref), refs(l_ref), refs(acc_ref))

    @pl.when(kb < nkb)
    def _():
        col = lax.broadcasted_iota(jnp.int32, (1, bk), 1)
        rel = (col + (kb * bk - past)).astype(F32)
        step(lambda i: _tile_rows(kc_ref, i, bk).astype(BF16),
             lambda h: jnp.concatenate([_tile_rows(vc_ref, half * heads + h, bk) for half in range(vd // LANES)],
                                       axis=1).astype(BF16),
             lambda slope2: slope2 * rel)

    @pl.when(kb == nkb)
    def _():
        step(lambda i: kn_ref[:, i * hd:(i + 1) * hd], lambda h: vn_ref[:, h * vd:(h + 1) * vd],
             lambda slope2: _da_same_block_bias(nq, nq, slope2))
        lam = _da_lambda(lam_ref, lam_init)
        for h in range(heads):
            o_ref[:, h * vd:(h + 1) * vd] = _da_finish(
                acc_ref[2 * h], l_ref[2 * h], acc_ref[2 * h + 1], l_ref[2 * h + 1],
                lam, g_ref[...], lam_init).astype(o_ref.dtype)


def _da_sample(q, kn, vn, kc, vc, slopes, lam_vecs, gain, batch, nq, past, heads, hd, vd, lam_init):
    assert nq == CHUNK and past % CHUNK == 0
    assert hd == LANES and 2 * heads == SUBLANES and heads * (vd // LANES) == SUBLANES
    qk_w = heads * 2 * hd
    v_w = heads * vd
    bk = min(DA_CACHE_BLOCK, past)
    assert past % bk == 0
    nkb = past // bk
    blocks = 2 * nq * qk_w * 2 + 2 * nq * v_w * 2 + bk * (qk_w + v_w) * 4
    scratch = 2 * 2 * heads * nq * LANES * 4 + 2 * heads * nq * vd * 4
    cache_idx = lambda b, j: (b * nkb + jnp.minimum(j, nkb - 1), 0)
    return pl.pallas_call(
        functools.partial(_da_sample_kernel, heads=heads, hd=hd, vd=vd, past=past, slopes=slopes,
                          lam_init=lam_init),
        grid=(batch, nkb + 1),
        in_specs=[pl.BlockSpec((nq, qk_w), lambda b, j: (b, 0)),
                  pl.BlockSpec((nq, qk_w), lambda b, j: (b, 0)),
                  pl.BlockSpec((nq, v_w), lambda b, j: (b, 0)),
                  pl.BlockSpec((bk * SUBLANES, LANES), cache_idx),
                  pl.BlockSpec((bk * SUBLANES, LANES), cache_idx),
                  pl.BlockSpec((4, hd), lambda b, j: (0, 0)),
                  pl.BlockSpec((1, vd), lambda b, j: (0, 0))],
        out_specs=pl.BlockSpec((nq, v_w), lambda b, j: (b, 0)),
        out_shape=jax.ShapeDtypeStruct(vn.shape, BF16),
        scratch_shapes=[pltpu.VMEM((2 * heads, nq, LANES), F32), pltpu.VMEM((2 * heads, nq, LANES), F32),
                        pltpu.VMEM((2 * heads, nq, vd), F32)],
        compiler_params=pltpu.CompilerParams(
            dimension_semantics=("arbitrary", "arbitrary"),
            vmem_limit_bytes=_vmem_limit(blocks, scratch)),
        name="da_sample",
    )(q, kn, vn, kc, vc, lam_vecs, gain)


def _route(logits, n_groups, per_group):
    n_exp = n_groups * per_group
    lane = lax.broadcasted_iota(jnp.int32, logits.shape, 1).astype(F32)
    big = jnp.float32(3e38)

    def first_argmax(x):
        top = jnp.max(x, axis=1, keepdims=True)
        idx = jnp.min(jnp.where(x == top, lane, big), axis=1, keepdims=True)
        return top, idx

    is_coarse = (lane >= n_exp) & (lane < n_exp + n_groups)
    coarse = jnp.where(is_coarse, logits, -big)
    c_top, c_idx = first_argmax(coarse)
    g_gate = 1.0 / jnp.sum(jnp.where(is_coarse, jnp.exp(logits - c_top), 0.0), axis=1, keepdims=True)
    g_first = (c_idx - n_exp) * per_group
    in_group = (lane >= g_first) & (lane < g_first + per_group)
    fine = jnp.where(in_group, logits, -big)
    v1, i1 = first_argmax(fine)
    v2, i2 = first_argmax(jnp.where(lane == i1, -big, fine))
    e2 = jnp.exp(v2 - v1)
    w1 = g_gate / (1.0 + e2)
    w2 = g_gate * e2 / (1.0 + e2)
    comb = jnp.where(lane == i1, w1, jnp.where(lane == i2, w2, 0.0))
    return jnp.where(lane == n_exp, c_idx - n_exp, comb)


def _post_attn_kernel(sb_ref, da_ref, x_ref, wsb_ref, wda_ref, g_ref, b_ref, wr_ref, br_ref,
                      h1_ref, comb_ref, *, alpha, n_groups, per_group):
    mixed = _dot(sb_ref[...], wsb_ref[...]) + _dot(da_ref[...], wda_ref[...])
    h1 = _layer_norm(alpha * x_ref[...] + mixed, g_ref[...], b_ref[...])
    h1_ref[...] = h1
    h_hi, h_lo = _split_bf16(h1)
    both = _dot(h_hi, wr_ref[...])
    logits = both[:, :LANES] + both[:, LANES:] + _dot(h_lo, wr_ref[:, :LANES]) + br_ref[...]
    comb_ref[...] = _route(logits, n_groups, per_group)


def _post_attn(sb_o, da_o, x, w_out, ln_g, ln_b, w_route, b_route, alpha, n_groups, per_group):
    t, d = x.shape
    w_sb = sb_o.shape[1]
    w_da = da_o.shape[1]
    assert w_sb == w_da
    tm = min(TOKEN_TILE, t)
    assert t % tm == 0
    blocks = (tm * (w_sb + w_da) * 2 + tm * d * 4 * 2 + (w_sb + w_da) * d * 2
              + d * 2 * LANES * 2 + tm * LANES * 4)
    row = lambda i: (i, 0)
    fixed = lambda i: (0, 0)
    return pl.pallas_call(
        functools.partial(_post_attn_kernel, alpha=alpha, n_groups=n_groups, per_group=per_group),
        grid=(t // tm,),
        in_specs=[pl.BlockSpec((tm, w_sb), row), pl.BlockSpec((tm, w_da), row), pl.BlockSpec((tm, d), row),
                  pl.BlockSpec((w_sb, d), fixed), pl.BlockSpec((w_da, d), lambda i: (1, 0)),
                  pl.BlockSpec((1, d), fixed), pl.BlockSpec((1, d), fixed),
                  pl.BlockSpec((d, 2 * LANES), fixed), pl.BlockSpec((1, LANES), fixed)],
        out_specs=[pl.BlockSpec((tm, d), row), pl.BlockSpec((tm, LANES), row)],
        out_shape=[jax.ShapeDtypeStruct((t, d), F32), jax.ShapeDtypeStruct((t, LANES), F32)],
        compiler_params=pltpu.CompilerParams(
            dimension_semantics=("arbitrary",), vmem_limit_bytes=_vmem_limit(blocks)),
        name="post_attn",
    )(sb_o, da_o, x, w_out, w_out, ln_g, ln_b, w_route, b_route)


def _dispatch_plan(comb, n_exp, n_groups, tm):
    t = comb.shape[0]
    per_group = n_exp // n_groups
    n_tiles = t // tm + n_groups
    group = comb[:, n_exp].astype(jnp.int32)
    active = comb[:, :n_exp] != 0.0
    active_bits = jnp.sum(active.astype(jnp.int32) << jnp.arange(n_exp, dtype=jnp.int32)[None, :], axis=1)
    counts = jnp.sum((group[:, None] == jnp.arange(n_groups)[None, :]).astype(jnp.int32), axis=0)
    tiles_g = (counts + tm - 1) // tm
    tile_end = jnp.cumsum(tiles_g)
    tile_start = tile_end - tiles_g
    count_start = jnp.cumsum(counts) - counts
    tile_ids = jnp.arange(n_tiles, dtype=jnp.int32)
    tile_group = jnp.minimum(jnp.sum((tile_ids[:, None] >= tile_end[None, :]).astype(jnp.int32), axis=1),
                             n_groups - 1)
    first_row = (tile_ids - tile_start[tile_group]) * tm
    tile_valid = jnp.clip(counts[tile_group] - first_row, 0, tm)
    order = jnp.argsort(group * (1 << n_exp) + active_bits, stable=True).astype(jnp.int32)
    slot_group = jnp.repeat(tile_group, tm)
    rank = jnp.repeat(first_row, tm) + jnp.tile(jnp.arange(tm, dtype=jnp.int32), n_tiles)
    rank = jnp.minimum(rank, counts[slot_group] - 1)
    src = order[jnp.clip(count_start[slot_group] + rank, 0, t - 1)]
    comb_sorted = comb[src]
    tile_active = jnp.any((comb_sorted[:, :n_exp] != 0.0).reshape(n_tiles, tm, n_exp), axis=1)
    in_group = jnp.take_along_axis(
        tile_active, tile_group[:, None] * per_group + jnp.arange(per_group)[None, :], axis=1)
    in_group = jnp.logical_and(in_group, (tile_ids < tile_end[-1])[:, None])
    first_needed = jnp.argsort(jnp.logical_not(in_group), axis=1, stable=True)
    need = (jnp.arange(per_group)[None, :] < jnp.sum(in_group.astype(jnp.int32), axis=1)[:, None]).reshape(-1)
    step_expert = (tile_group[:, None] * per_group + first_needed).reshape(-1)
    steps = jnp.arange(n_tiles * per_group, dtype=jnp.int32)
    last_needed = lax.cummax(jnp.where(need, steps, -1), axis=0)
    resident = step_expert[jnp.maximum(last_needed, 0)]
    return (src, tile_valid.astype(jnp.int32), tile_end[-1:].astype(jnp.int32),
            need.astype(jnp.int32), resident.astype(jnp.int32), comb_sorted)


def _moe_kernel(src_ref, valid_ref, used_ref, need_ref, resident_ref,
                h1_hbm, comb_ref, wg_ref, wu_ref, wd_ref, g_ref, b_ref,
                y_hbm, xbuf, xb_ref, acc_ref, obuf, sem_in, sem_out, *, alpha, tm, per_group):
    i = pl.program_id(0)
    j = pl.program_id(1)
    n_used = used_ref[0]
    slot = lax.rem(i, 2)
    live = i < n_used

    def row_in(tile, r, buf_slot):
        return pltpu.make_async_copy(h1_hbm.at[pl.ds(src_ref[tile * tm + r], 1), :],
                                     xbuf.at[buf_slot, pl.ds(r, 1), :], sem_in.at[buf_slot])

    def row_out(tile, r):
        return pltpu.make_async_copy(obuf.at[pl.ds(r, 1), :],
                                     y_hbm.at[pl.ds(src_ref[tile * tm + r], 1), :], sem_out)

    def gather(tile, buf_slot):
        def issue(r, c):
            row_in(tile, r, buf_slot).start()
            return c
        lax.fori_loop(0, tm, issue, 0, unroll=8)

    def wait_gather(buf_slot):
        pltpu.make_async_copy(h1_hbm.at[pl.ds(0, tm), :], xbuf.at[buf_slot], sem_in.at[buf_slot]).wait()

    def drain_scatter(tile):
        n = valid_ref[tile]

        def wait_chunk(c, carry):
            pltpu.make_async_copy(obuf.at[pl.ds(0, SCATTER_WAIT_ROWS), :],
                                  y_hbm.at[pl.ds(0, SCATTER_WAIT_ROWS), :], sem_out).wait()
            return carry

        def wait_row(r, carry):
            row_out(tile, 0).wait()
            return carry
        lax.fori_loop(0, n // SCATTER_WAIT_ROWS, wait_chunk, 0)
        lax.fori_loop(0, lax.rem(n, SCATTER_WAIT_ROWS), wait_row, 0)

    def expert_step():
        xb = xb_ref[...]
        gate = _dot(xb, wg_ref[...])
        up = _dot(xb, wu_ref[...])
        hidden = (gate / (1.0 + jnp.exp(-gate))) * up
        lane = lax.broadcasted_iota(jnp.int32, comb_ref.shape, 1)
        expert = resident_ref[i * per_group + j]
        weight = jnp.sum(jnp.where(lane == expert, comb_ref[...], 0.0), axis=1, keepdims=True)
        return weight * _dot(hidden.astype(BF16), wd_ref[...])

    @pl.when(jnp.logical_and(live, j == 0))
    def _():
        @pl.when(i == 0)
        def _():
            gather(0, 0)
        wait_gather(slot)
        xb_ref[...] = xbuf[slot].astype(BF16)
        nxt = jnp.minimum(i + 1, n_used - 1)
        for r in range(tm):
            row_in(nxt, r, 1 - slot).start()
        acc_ref[...] = expert_step()

    @pl.when(jnp.logical_and(jnp.logical_and(live, j > 0), need_ref[i * per_group + j] > 0))
    def _():
        acc_ref[...] += expert_step()

    @pl.when(jnp.logical_and(live, j == per_group - 1))
    def _():
        @pl.when(i > 0)
        def _():
            drain_scatter(i - 1)
        obuf[...] = _layer_norm(alpha * xbuf[slot] + acc_ref[...], g_ref[...], b_ref[...])

        n = valid_ref[i]

        def issue_group(c, carry):
            for r in range(SCATTER_UNROLL):
                row_out(i, c * SCATTER_UNROLL + r).start()
            return carry

        def issue_row(r, carry):
            row_out(i, (n // SCATTER_UNROLL) * SCATTER_UNROLL + r).start()
            return carry
        @pl.when(n == tm)
        def _():
            for r in range(tm):
                row_out(i, r).start()

        @pl.when(n < tm)
        def _():
            lax.fori_loop(0, n // SCATTER_UNROLL, issue_group, 0)
            lax.fori_loop(0, lax.rem(n, SCATTER_UNROLL), issue_row, 0)

        @pl.when(i == n_used - 1)
        def _():
            drain_scatter(i)
            wait_gather(1 - slot)


def _moe(h1, comb, w_gate, w_up, w_down, ln_g, ln_b, alpha, n_groups):
    t, d = h1.shape
    n_exp, _, d_e = w_gate.shape
    per_group = n_exp // n_groups
    tm = min(TOKEN_TILE if t >= 4 * n_groups * TOKEN_TILE else TOKEN_TILE // 2, t)
    assert t % tm == 0
    src, tile_valid, n_used, need, resident, comb_sorted = _dispatch_plan(comb, n_exp, n_groups, tm)
    n_tiles = tile_valid.shape[0]
    blocks = tm * LANES * 4 + 3 * d * d_e * 2
    scratch = 2 * tm * d * 4 + tm * d * 2 + 2 * tm * d * 4
    expert_idx = lambda i, j, src, valid, used, need, res: (res[i * per_group + j], 0, 0)
    fixed = lambda i, j, *prefetch: (0, 0)
    return pl.pallas_call(
        functools.partial(_moe_kernel, alpha=alpha, tm=tm, per_group=per_group),
        grid_spec=pltpu.PrefetchScalarGridSpec(
            num_scalar_prefetch=5,
            grid=(n_tiles, per_group),
            in_specs=[pl.BlockSpec(memory_space=pl.ANY),
                      pl.BlockSpec((tm, LANES), lambda i, j, *prefetch: (i, 0)),
                      pl.BlockSpec((None, d, d_e), expert_idx),
                      pl.BlockSpec((None, d, d_e), expert_idx),
                      pl.BlockSpec((None, d_e, d), expert_idx),
                      pl.BlockSpec((1, d), fixed), pl.BlockSpec((1, d), fixed)],
            out_specs=pl.BlockSpec(memory_space=pl.ANY),
            scratch_shapes=[pltpu.VMEM((2, tm, d), F32), pltpu.VMEM((tm, d), BF16), pltpu.VMEM((tm, d), F32),
                            pltpu.VMEM((tm, d), F32), pltpu.SemaphoreType.DMA((2,)), pltpu.SemaphoreType.DMA]),
        out_shape=jax.ShapeDtypeStruct((t, d), F32),
        compiler_params=pltpu.CompilerParams(
            dimension_semantics=("arbitrary", "arbitrary"),
            vmem_limit_bytes=_vmem_limit(blocks, scratch)),
        name="moe",
    )(src, tile_valid, n_used, need, resident, h1, comb_sorted, w_gate, w_up, w_down, ln_g, ln_b)


def kernel(x_prompt, x_sample, cache_sb_k, cache_sb_v, cache_da_k, cache_da_v, w_in, w_out, lambda_q1, lambda_k1, lambda_q2, lambda_k2, subln_g, ln1_g, ln1_b, w_coarse, b_coarse, w_fine, b_fine, w_gate, w_up, w_down, ln2_g, ln2_b):
    depth, dec_batch, past, sb_heads, sb_hd = cache_sb_k.shape
    _, _, _, da_heads, _, da_hd = cache_da_k.shape
    da_vd = cache_da_v.shape[-1]
    batch, seq, d_model = x_prompt.shape
    dec_seq = x_sample.shape[1]
    n_groups, per_group = w_fine.shape[2], w_fine.shape[3]
    n_exp = n_groups * per_group
    assert n_exp + n_groups <= LANES
    sb_w = sb_heads * sb_hd
    assert sb_w == da_heads * 2 * da_hd == da_heads * da_vd == SUBLANES * LANES, \
        "projection column groups are addressed as equal-width blocks of one token tile"
    alpha = (2 * depth) ** 0.25
    slopes = tuple(2.0 ** (-8.0 * (h + 1) / da_heads) for h in range(da_heads))
    slopes_arr = jnp.asarray(slopes, F32)

    hp = x_prompt.reshape(batch * seq, d_model)
    hs = x_sample.reshape(dec_batch * dec_seq, d_model)
    rows_p, rows_s = [], []
    for l in range(depth):
        lam_init = 0.8 - 0.6 * math.exp(-0.3 * l)
        w_in_b = w_in[l].astype(BF16)
        w_out_b = w_out[l].astype(BF16)
        lam_vecs = jnp.stack([lambda_q1[l], lambda_k1[l], lambda_q2[l], lambda_k2[l]]).astype(F32)
        gain = subln_g[l].reshape(1, da_vd)
        w_route = jnp.concatenate(
            [w_fine[l].reshape(d_model, n_exp), w_coarse[l],
             jnp.zeros((d_model, LANES - n_exp - n_groups), F32)], axis=1)
        w_route_b = jnp.concatenate(_split_bf16(w_route), axis=1)
        b_route = jnp.concatenate(
            [b_fine[l].reshape(n_exp), b_coarse[l], jnp.zeros((LANES - n_exp - n_groups,), F32)]).reshape(1, LANES)
        wg_b, wu_b, wd_b = w_gate[l].astype(BF16), w_up[l].astype(BF16), w_down[l].astype(BF16)
        g1, b1 = ln1_g[l].reshape(1, d_model), ln1_b[l].reshape(1, d_model)
        g2, b2 = ln2_g[l].reshape(1, d_model), ln2_b[l].reshape(1, d_model)

        halves = da_vd // LANES
        dv_slots = tuple(h * halves + half for half in range(halves) for h in range(da_heads))
        plain_slots = tuple(range(SUBLANES))

        sb_scale = sb_hd ** -0.5 * LOG2E
        da_scale = da_hd ** -0.5 * LOG2E

        def project(x, tag):
            sq, sk_b, sv_b, sk, sv = _proj(
                x, w_in_b, 0, ((sb_scale, None), (None, plain_slots), (None, plain_slots)), "proj_sb_" + tag)
            dq, dk_b, dv_b, dk, dv = _proj(
                x, w_in_b, 3, ((da_scale, None), (None, plain_slots), (None, dv_slots)), "proj_da_" + tag)
            return (sq, sk_b, sv_b, dq, dk_b, dv_b), (sk, sv, dk, dv)

        def dv_tiles_to_rows(tiles, lead):
            return tiles.reshape(lead + (halves, da_heads, LANES)).swapaxes(-3, -2).reshape(lead + (da_heads, da_vd))

        def dv_rows_to_tiles(rows_, n):
            return rows_.reshape(n, da_heads, halves, LANES).swapaxes(1, 2).reshape(n * SUBLANES, LANES)

        (sq, sk, sv, dq, dk, dv), new_p = project(hp, "p")
        sb_o = _sb_prompt(sq, sk, sv, batch, seq, sb_heads, sb_hd)
        da_o = _da_prompt(dq, dk, dv, slopes_arr, lam_vecs, gain, batch, seq, da_heads, da_hd, da_vd, lam_init)
        h1, comb = _post_attn(sb_o, da_o, hp, w_out_b, g1, b1, w_route_b, b_route, alpha, n_groups, per_group)
        hp = _moe(h1, comb, wg_b, wu_b, wd_b, g2, b2, alpha, n_groups)

        (sq, sk, sv, dq, dk, dv), new_s = project(hs, "s")
        rows = dec_batch * past
        sb_o = _sb_sample(sq, sk, sv, cache_sb_k[l].reshape(rows * SUBLANES, LANES),
                          cache_sb_v[l].reshape(rows * SUBLANES, LANES),
                          dec_batch, dec_seq, past, sb_heads, sb_hd)
        da_o = _da_sample(dq, dk, dv, cache_da_k[l].reshape(rows * SUBLANES, LANES),
                          dv_rows_to_tiles(cache_da_v[l], rows),
                          slopes, lam_vecs, gain, dec_batch, dec_seq, past, da_heads, da_hd, da_vd, lam_init)
        h1, comb = _post_attn(sb_o, da_o, hs, w_out_b, g1, b1, w_route_b, b_route, alpha, n_groups, per_group)
        hs = _moe(h1, comb, wg_b, wu_b, wd_b, g2, b2, alpha, n_groups)
        for new, lead, rows_out in ((new_p, (batch, seq), rows_p), (new_s, (dec_batch, dec_seq), rows_s)):
            rows_out.append((new[0].reshape(lead + (sb_heads, sb_hd)), new[1].reshape(lead + (sb_heads, sb_hd)),
                             new[2].reshape(lead + (da_heads, 2, da_hd)), dv_tiles_to_rows(new[3], lead)))

    def stack(rows, i):
        return rows[0][i][None] if len(rows) == 1 else jnp.stack([r[i] for r in rows], axis=0)

    return (hp.reshape(batch, seq, d_model), hs.reshape(dec_batch, dec_seq, d_model),
            *(stack(rows_p, i) for i in range(4)), *(stack(rows_s, i) for i in range(4)))
```
